```python
import math
import jax, jax.numpy as jnp
from jax import lax
import numpy as np

D_MODEL = 2048
BATCH = 4
SEQ = 2048
DEPTH = 4
DEC_BATCH = 128
DEC_SEQ = 4
PAST_LEN = 16384
PAGE_SIZE = 128

N_AB = (DEPTH + 1) // 2
N_C = DEPTH // 2
CONV_W = 4
CHUNK = 64
EPS = 1e-6
TINY = 1e-30

SSM_D_INNER = D_MODEL
SSM_HEAD_DIM = 64
SSM_HEADS = SSM_D_INNER // SSM_HEAD_DIM
SSM_GROUPS = 4
SSM_D_STATE = 128
SSM_CONV_DIM = SSM_D_INNER + 2 * SSM_GROUPS * SSM_D_STATE

HG_WIDTH = D_MODEL
HG_HEAD_DIM = 128
HG_HEADS = HG_WIDTH // HG_HEAD_DIM

AB_IN = SSM_D_INNER + SSM_CONV_DIM + SSM_HEADS + 4 * HG_WIDTH
AB_MIX = SSM_D_INNER + HG_WIDTH

GDN_HEAD_K = 128
GDN_HEAD_V = 128
GDN_K_HEADS = D_MODEL // GDN_HEAD_K
GDN_V_HEADS = 2 * GDN_K_HEADS
GDN_KEY_DIM = GDN_K_HEADS * GDN_HEAD_K
GDN_VAL_DIM = GDN_V_HEADS * GDN_HEAD_V
GDN_CONV_DIM = 2 * GDN_KEY_DIM + GDN_VAL_DIM
C_IN = GDN_CONV_DIM + GDN_VAL_DIM + 2 * GDN_V_HEADS

FFN_HIDDEN = 4 * D_MODEL

kernel_name = 'hybrid_ssd_hgrn2_gdn_step'


def rmsnorm(x, w):
    xf = x.astype(jnp.float32)
    y = xf * lax.rsqrt(jnp.mean(xf * xf, axis=-1, keepdims=True) + EPS)
    return (y * w.astype(jnp.float32)).astype(x.dtype)


def group_rmsnorm(x, w, n_groups):
    shp = x.shape
    xf = x.astype(jnp.float32).reshape(shp[:-1] + (n_groups, shp[-1] // n_groups))
    xf = xf * lax.rsqrt(jnp.mean(xf * xf, axis=-1, keepdims=True) + EPS)
    return xf.reshape(shp) * w.astype(jnp.float32)


def l2norm(x):
    return x * lax.rsqrt(jnp.sum(x * x, axis=-1, keepdims=True) + EPS)


def split_cols(a, sizes):
    out, start = [], 0
    for s in sizes:
        out.append(a[..., start:start + s])
        start += s
    return out


def masked_exp(d, mask):
    return jnp.where(mask, jnp.exp(jnp.where(mask, d, 0.0)), 0.0)


def causal_conv(x, buf, w, b=None):
    L = x.shape[1]
    xp = jnp.concatenate([buf.astype(x.dtype), x], axis=1)
    y = xp[:, 0:L] * w[0]
    for j in range(1, CONV_W):
        y = y + xp[:, j:j + L] * w[j]
    if b is not None:
        y = y + b
    return y, xp[:, L:]


def chunk_layout(L):
    c = min(CHUNK, L)
    return c, -(-L // c)


def to_chunks(a, c, n):
    L = a.shape[1]
    a = jnp.pad(a, [(0, 0), (0, n * c - L)] + [(0, 0)] * (a.ndim - 2))
    a = a.reshape((a.shape[0], n, c) + a.shape[2:])
    return jnp.moveaxis(a, 1, 0)


def from_chunks(y, L):
    y = jnp.moveaxis(y, 0, 1)
    y = y.reshape((y.shape[0], -1) + y.shape[3:])
    return y[:, :L]


def ssd_scan(x, dt, a_neg, bm, cm, S0):
    Bsz, L, H, P = x.shape
    G, N = bm.shape[2], bm.shape[3]
    R = H // G
    c, n = chunk_layout(L)
    tri = jnp.tril(jnp.ones((c, c), dtype=bool))[None, :, :, None, None]
    xs = to_chunks((x * dt[..., None]).reshape(Bsz, L, G, R, P), c, n)
    la = to_chunks((dt * a_neg).reshape(Bsz, L, G, R), c, n)
    bc = to_chunks(bm, c, n)
    cc = to_chunks(cm, c, n)

    def step(S, inp):
        xi, ai, bi, ci = inp
        acum = jnp.cumsum(ai, axis=1)
        decay = masked_exp(acum[:, :, None] - acum[:, None, :], tri)
        cb = jnp.einsum('btgn,bsgn->btsg', ci, bi)
        y = jnp.einsum('btsg,btsgr,bsgrp->btgrp', cb, decay, xi)
        y = y + jnp.einsum('btgn,bgrpn,btgr->btgrp', ci, S, jnp.exp(acum))
        alast = acum[:, -1]
        S = jnp.exp(alast)[..., None, None] * S + jnp.einsum(
            'bsgn,bsgr,bsgrp->bgrpn', bi, jnp.exp(alast[:, None] - acum), xi)
        return S, y

    S, ys = lax.scan(step, S0.reshape(Bsz, G, R, P, N), (xs, la, bc, cc))
    return from_chunks(ys, L).reshape(Bsz, L, H, P), S.reshape(Bsz, H, P, N)


def gla_scan(q, k, v, logf, S0):
    L = q.shape[1]
    c, n = chunk_layout(L)
    tri = jnp.tril(jnp.ones((c, c), dtype=bool))[None, :, :, None, None]
    inputs = tuple(to_chunks(a, c, n) for a in (q, k, v, logf))

    def step(S, inp):
        qi, ki, vi, li = inp
        g = jnp.cumsum(li, axis=1)
        decay = masked_exp(g[:, :, None] - g[:, None, :], tri)
        att = jnp.sum(qi[:, :, None] * ki[:, None] * decay, axis=-1)
        y = jnp.einsum('btsh,bshv->bthv', att, vi) + jnp.einsum('bthk,bhkv->bthv', qi * jnp.exp(g), S)
        glast = g[:, -1]
        S = jnp.exp(glast)[..., None] * S + jnp.einsum(
            'bshk,bshv->bhkv', ki * jnp.exp(glast[:, None] - g), vi)
        return S, y

    S, ys = lax.scan(step, S0, inputs)
    return from_chunks(ys, L), S


def gated_delta_scan(q, k, v, g, beta, S0):
    L = q.shape[1]
    c, n = chunk_layout(L)
    tri = jnp.tril(jnp.ones((c, c), dtype=bool))
    strict = jnp.tril(jnp.ones((c, c), dtype=bool), k=-1)
    eye = jnp.eye(c, dtype=jnp.float32)
    inputs = tuple(to_chunks(a, c, n) for a in (q, k, v, g, beta))

    def step(S, inp):
        qi, ki, vi, gi, bi = inp
        gcum = jnp.cumsum(gi, axis=1)
        gh = jnp.moveaxis(gcum, 1, 2)
        gam = masked_exp(gh[..., :, None] - gh[..., None, :], tri)
        kb = ki * bi[..., None]
        a_mat = jnp.where(strict, jnp.einsum('bthk,bshk->bhts', kb, ki) * gam, 0.0)
        t_mat = lax.linalg.triangular_solve(eye + a_mat, jnp.broadcast_to(eye, a_mat.shape),
                                            left_side=True, lower=True, unit_diagonal=True)
        w = jnp.einsum('bhts,bshk->bthk', t_mat, kb * jnp.exp(gcum)[..., None])
        u = jnp.einsum('bhts,bshv->bthv', t_mat, vi * bi[..., None])
        v_new = u - jnp.einsum('bthk,bhkv->bthv', w, S)
        qk = jnp.einsum('bthk,bshk->bhts', qi, ki) * gam
        y = jnp.einsum('bthk,bhkv->bthv', qi * jnp.exp(gcum)[..., None], S) + \
            jnp.einsum('bhts,bshv->bthv', qk, v_new)
        glast = gcum[:, -1]
        S = jnp.exp(glast)[..., None, None] * S + jnp.einsum(
            'bshk,bshv->bhkv', ki * jnp.exp(glast[:, None] - gcum)[..., None], v_new)
        return S, y

    S, ys = lax.scan(step, S0, inputs)
    return from_chunks(ys, L), S


def ab_mixer(h, S_ssm, buf_ssm, S_hg, lb, w_in, conv_w, conv_b, dt_bias, A_log, D_skip,
             ssm_norm_w, hg_norm_w, w_out):
    f32 = jnp.float32
    Bsz, L, _ = h.shape
    z, xbc, dt, hq, hf, hi, hg = split_cols(
        h @ w_in, [SSM_D_INNER, SSM_CONV_DIM, SSM_HEADS, HG_WIDTH, HG_WIDTH, HG_WIDTH, HG_WIDTH])
    xbc, new_buf = causal_conv(xbc, buf_ssm, conv_w, conv_b)
    xbc = jax.nn.silu(xbc.astype(f32))
    xs, bm, cm = split_cols(xbc, [SSM_D_INNER, SSM_GROUPS * SSM_D_STATE, SSM_GROUPS * SSM_D_STATE])
    xs = xs.reshape(Bsz, L, SSM_HEADS, SSM_HEAD_DIM)
    bm = bm.reshape(Bsz, L, SSM_GROUPS, SSM_D_STATE)
    cm = cm.reshape(Bsz, L, SSM_GROUPS, SSM_D_STATE)
    dt = jax.nn.softplus(dt.astype(f32) + dt_bias.astype(f32))
    a_neg = -jnp.exp(A_log.astype(f32))
    y, S_ssm_new = ssd_scan(xs, dt, a_neg, bm, cm, S_ssm.astype(f32))
    y = (y + D_skip.astype(f32)[:, None] * xs).reshape(Bsz, L, SSM_D_INNER)
    y = group_rmsnorm(y * jax.nn.silu(z.astype(f32)), ssm_norm_w, SSM_GROUPS)
    sig = jax.nn.sigmoid(hf.astype(f32))
    heads = lambda a: a.reshape(Bsz, L, HG_HEADS, HG_HEAD_DIM)
    q = heads(jax.nn.silu(hq.astype(f32)))
    f = lb + (1.0 - lb) * sig
    logf = jnp.log(jnp.maximum(f, TINY))
    k = (1.0 - lb) * (1.0 - sig)
    o, S_hg_new = gla_scan(q, heads(k), heads(hi.astype(f32)), heads(logf), S_hg.astype(f32))
    o = group_rmsnorm(o.reshape(Bsz, L, HG_WIDTH), hg_norm_w, HG_HEADS) * jax.nn.silu(hg.astype(f32))
    out = jnp.concatenate([y, o], axis=-1).astype(h.dtype) @ w_out
    return out, S_ssm_new, new_buf, S_hg_new


def gdn_mixer(h, S, buf, w_in, conv_w, dt_bias, A_log, norm_w, w_out):
    f32 = jnp.float32
    Bsz, L, _ = h.shape
    qkv, z, b, a = split_cols(h @ w_in, [GDN_CONV_DIM, GDN_VAL_DIM, GDN_V_HEADS, GDN_V_HEADS])
    qkv, new_buf = causal_conv(qkv, buf, conv_w)
    qkv = jax.nn.silu(qkv.astype(f32))
    q, k, v = split_cols(qkv, [GDN_KEY_DIM, GDN_KEY_DIM, GDN_VAL_DIM])
    rep = GDN_V_HEADS // GDN_K_HEADS
    q = jnp.repeat(l2norm(q.reshape(Bsz, L, GDN_K_HEADS, GDN_HEAD_K)), rep, axis=2) * (GDN_HEAD_K ** -0.5)
    k = jnp.repeat(l2norm(k.reshape(Bsz, L, GDN_K_HEADS, GDN_HEAD_K)), rep, axis=2)
    v = v.reshape(Bsz, L, GDN_V_HEADS, GDN_HEAD_V)
    beta = jax.nn.sigmoid(b.astype(f32))
    g = -jnp.exp(A_log.astype(f32)) * jax.nn.softplus(a.astype(f32) + dt_bias.astype(f32))
    o, S_new = gated_delta_scan(q, k, v, g, beta, S.astype(f32))
    o = rmsnorm(o, norm_w) * jax.nn.silu(z.astype(f32).reshape(Bsz, L, GDN_V_HEADS, GDN_HEAD_V))
    out = o.reshape(Bsz, L, GDN_VAL_DIM).astype(h.dtype) @ w_out
    return out, S_new, new_buf


def trunk(x, st_ssm, st_ssm_conv, st_hg, st_gdn, st_gdn_conv, p):
    sm = jax.nn.softmax(p['hg_lower_bounds'].astype(jnp.float32), axis=0)
    lbs = jnp.cumsum(sm, axis=0) - sm[0]
    n_ssm, n_ssm_conv, n_hg, n_gdn, n_gdn_conv = [], [], [], [], []
    for l in range(DEPTH):
        i = l // 2
        h = rmsnorm(x, p['norm_mix'][l])
        if l % 2 == 0:
            out, s1, b1, s2 = ab_mixer(h, st_ssm[i], st_ssm_conv[i], st_hg[i], lbs[i],
                                       p['w_in_ab'][i], p['ssm_conv_w'][i], p['ssm_conv_b'][i],
                                       p['ssm_dt_bias'][i], p['ssm_A_log'][i], p['ssm_D'][i],
                                       p['ssm_norm_w'][i], p['hg_norm_w'][i], p['w_out_ab'][i])
            n_ssm.append(s1); n_ssm_conv.append(b1); n_hg.append(s2)
        else:
            out, s3, b3 = gdn_mixer(h, st_gdn[i], st_gdn_conv[i], p['w_in_c'][i], p['gdn_conv_w'][i],
                                    p['gdn_dt_bias'][i], p['gdn_A_log'][i], p['gdn_norm_w'][i],
                                    p['w_out_c'][i])
            n_gdn.append(s3); n_gdn_conv.append(b3)
        x = x + out.astype(x.dtype)
        h = rmsnorm(x, p['norm_mlp'][l])
        x = x + jnp.square(jax.nn.relu(h @ p['w_ff1'][l])) @ p['w_ff2'][l]
    y = rmsnorm(x, p['norm_final'])
    return y, jnp.stack(n_ssm), jnp.stack(n_ssm_conv), jnp.stack(n_hg), jnp.stack(n_gdn), jnp.stack(n_gdn_conv)


def setup_inputs(seed: int = 0) -> dict:
    key = jax.random.key(seed)
    ks = iter(jax.random.split(key, 40))
    f32 = jnp.float32

    def nrm(shape, scale):
        return jax.random.normal(next(ks), shape, f32) * scale

    def gain(shape):
        return 1.0 + nrm(shape, 0.02)

    def dt_bias(shape):
        dt = jnp.exp(jax.random.uniform(next(ks), shape, f32, math.log(1e-3), math.log(1e-1)))
        return dt + jnp.log(-jnp.expm1(-dt))

    def a_log(shape):
        return jnp.log(jax.random.uniform(next(ks), shape, f32, 1.0, 16.0))

    return {
        'x_prompt': nrm((BATCH, SEQ, D_MODEL), 1.0),
        'x_sample': nrm((DEC_BATCH, DEC_SEQ, D_MODEL), 1.0),
        'state_ssm': nrm((N_AB, DEC_BATCH, SSM_HEADS, SSM_HEAD_DIM, SSM_D_STATE), 0.1),
        'state_ssm_conv': nrm((N_AB, DEC_BATCH, CONV_W - 1, SSM_CONV_DIM), 1.0),
        'state_hgrn': nrm((N_AB, DEC_BATCH, HG_HEADS, HG_HEAD_DIM, HG_HEAD_DIM), 0.1),
        'state_gdn': nrm((N_C, DEC_BATCH, GDN_V_HEADS, GDN_HEAD_K, GDN_HEAD_V), 0.1),
        'state_gdn_conv': nrm((N_C, DEC_BATCH, CONV_W - 1, GDN_CONV_DIM), 1.0),
        'norm_mix': gain((DEPTH, D_MODEL)),
        'norm_mlp': gain((DEPTH, D_MODEL)),
        'norm_final': gain((D_MODEL,)),
        'w_in_ab': nrm((N_AB, D_MODEL, AB_IN), D_MODEL ** -0.5),
        'ssm_conv_w': nrm((N_AB, CONV_W, SSM_CONV_DIM), CONV_W ** -0.5),
        'ssm_conv_b': nrm((N_AB, SSM_CONV_DIM), 0.02),
        'ssm_dt_bias': dt_bias((N_AB, SSM_HEADS)),
        'ssm_A_log': a_log((N_AB, SSM_HEADS)),
        'ssm_D': gain((N_AB, SSM_HEADS)),
        'ssm_norm_w': gain((N_AB, SSM_D_INNER)),
        'hg_lower_bounds': nrm((N_AB, HG_WIDTH), 0.5),
        'hg_norm_w': gain((N_AB, HG_WIDTH)),
        'w_out_ab': nrm((N_AB, AB_MIX, D_MODEL), AB_MIX ** -0.5),
        'w_in_c': nrm((N_C, D_MODEL, C_IN), D_MODEL ** -0.5),
        'gdn_conv_w': nrm((N_C, CONV_W, GDN_CONV_DIM), CONV_W ** -0.5),
        'gdn_dt_bias': dt_bias((N_C, GDN_V_HEADS)),
        'gdn_A_log': a_log((N_C, GDN_V_HEADS)),
        'gdn_norm_w': gain((N_C, GDN_HEAD_V)),
        'w_out_c': nrm((N_C, GDN_VAL_DIM, D_MODEL), GDN_VAL_DIM ** -0.5),
        'w_ff1': nrm((DEPTH, D_MODEL, FFN_HIDDEN), D_MODEL ** -0.5),
        'w_ff2': nrm((DEPTH, FFN_HIDDEN, D_MODEL), FFN_HIDDEN ** -0.5),
    }


def reference(x_prompt, x_sample, state_ssm, state_ssm_conv, state_hgrn, state_gdn, state_gdn_conv,
              norm_mix, norm_mlp, norm_final, w_in_ab, ssm_conv_w, ssm_conv_b, ssm_dt_bias, ssm_A_log,
              ssm_D, ssm_norm_w, hg_lower_bounds, hg_norm_w, w_out_ab, w_in_c, gdn_conv_w, gdn_dt_bias,
              gdn_A_log, gdn_norm_w, w_out_c, w_ff1, w_ff2):
    p = dict(norm_mix=norm_mix, norm_mlp=norm_mlp, norm_final=norm_final, w_in_ab=w_in_ab,
             ssm_conv_w=ssm_conv_w, ssm_conv_b=ssm_conv_b, ssm_dt_bias=ssm_dt_bias, ssm_A_log=ssm_A_log,
             ssm_D=ssm_D, ssm_norm_w=ssm_norm_w, hg_lower_bounds=hg_lower_bounds, hg_norm_w=hg_norm_w,
             w_out_ab=w_out_ab, w_in_c=w_in_c, gdn_conv_w=gdn_conv_w, gdn_dt_bias=gdn_dt_bias,
             gdn_A_log=gdn_A_log, gdn_norm_w=gdn_norm_w, w_out_c=w_out_c, w_ff1=w_ff1, w_ff2=w_ff2)
    bp = x_prompt.shape[0]
    f32 = jnp.float32
    z_ssm = jnp.zeros((N_AB, bp, SSM_HEADS, SSM_HEAD_DIM, SSM_D_STATE), f32)
    z_ssm_conv = jnp.zeros((N_AB, bp, CONV_W - 1, SSM_CONV_DIM), x_prompt.dtype)
    z_hg = jnp.zeros((N_AB, bp, HG_HEADS, HG_HEAD_DIM, HG_HEAD_DIM), f32)
    z_gdn = jnp.zeros((N_C, bp, GDN_V_HEADS, GDN_HEAD_K, GDN_HEAD_V), f32)
    z_gdn_conv = jnp.zeros((N_C, bp, CONV_W - 1, GDN_CONV_DIM), x_prompt.dtype)
    y_prompt, p_ssm, p_ssm_conv, p_hg, p_gdn, p_gdn_conv = trunk(
        x_prompt, z_ssm, z_ssm_conv, z_hg, z_gdn, z_gdn_conv, p)
    y_sample, s_ssm, s_ssm_conv, s_hg, s_gdn, s_gdn_conv = trunk(
        x_sample, state_ssm, state_ssm_conv, state_hgrn, state_gdn, state_gdn_conv, p)
    return (y_prompt, y_sample, p_ssm, p_ssm_conv, p_hg, p_gdn, p_gdn_conv,
            s_ssm, s_ssm_conv, s_hg, s_gdn, s_gdn_conv)
```

```python
import functools
import math

import jax
import jax.numpy as jnp
from jax import lax
from jax.experimental import pallas as pl
from jax.experimental.pallas import tpu as pltpu

F32 = jnp.float32
BF16 = jnp.bfloat16
HIGHEST = lax.Precision.HIGHEST

D_MODEL = 2048
DEPTH = 4
N_AB = (DEPTH + 1) // 2
N_C = DEPTH // 2
CONV_W = 4
CHUNK = 64
EPS = 1e-6
TINY = 1e-30

SSM_D_INNER = D_MODEL
SSM_HEAD_DIM = 64
SSM_HEADS = SSM_D_INNER // SSM_HEAD_DIM
SSM_GROUPS = 4
SSM_D_STATE = 128
SSM_CONV_DIM = SSM_D_INNER + 2 * SSM_GROUPS * SSM_D_STATE

HG_WIDTH = D_MODEL
HG_HEAD_DIM = 128
HG_HEADS = HG_WIDTH // HG_HEAD_DIM

GDN_HEAD_K = 128
GDN_HEAD_V = 128
GDN_K_HEADS = D_MODEL // GDN_HEAD_K
GDN_V_HEADS = 2 * GDN_K_HEADS
GDN_KEY_DIM = GDN_K_HEADS * GDN_HEAD_K
GDN_VAL_DIM = GDN_V_HEADS * GDN_HEAD_V
GDN_CONV_DIM = 2 * GDN_KEY_DIM + GDN_VAL_DIM

FFN_HIDDEN = 4 * D_MODEL

AB_OFF_XBC = SSM_D_INNER
AB_OFF_DT = AB_OFF_XBC + SSM_CONV_DIM
AB_OFF_HG = AB_OFF_DT + SSM_HEADS
C_OFF_Z = GDN_CONV_DIM
C_OFF_BA = C_OFF_Z + GDN_VAL_DIM

SUBLANES = 8
VMEM_LIMIT = 52 * 1024 * 1024


def _params(sem):
    return pltpu.CompilerParams(dimension_semantics=sem, vmem_limit_bytes=VMEM_LIMIT)


def _dot(a, b):
    return jnp.dot(a.astype(BF16), b.astype(BF16), preferred_element_type=F32)


def _dot_nt(a, b):
    return lax.dot_general(a.astype(BF16), b.astype(BF16), (((1,), (1,)), ((), ())),
                           preferred_element_type=F32)


def _dot_tn(a, b):
    return lax.dot_general(a.astype(BF16), b.astype(BF16), (((0,), (0,)), ((), ())),
                           preferred_element_type=F32)


def _dot_f32(a, b):
    return jnp.dot(a, b, precision=HIGHEST, preferred_element_type=F32)


def _sigmoid(x):
    return 1.0 / (1.0 + jnp.exp(-x))


def _silu(x):
    return x * _sigmoid(x)


def _softplus(x):
    return jnp.maximum(x, 0.0) + jnp.log1p(jnp.exp(-jnp.abs(x)))


def _iota2(shape, dim):
    return lax.broadcasted_iota(jnp.int32, shape, dim)


def _causal_conv(xpad_ref, x_ref, buf_ref, w_ref, c):
    @pl.when(pl.program_id(1) == 0)
    def _():
        xpad_ref[5:8, :] = buf_ref[...]

    xpad_ref[8:8 + c, :] = x_ref[...]
    w = w_ref[...]
    y = xpad_ref[5:5 + c, :] * w[0:1, :]
    for j in range(1, CONV_W):
        y = y + xpad_ref[5 + j:5 + j + c, :] * w[j:j + 1, :]
    return y


def _conv_carry(xpad_ref, bufout_ref, c, valid_last):
    @pl.when(pl.program_id(1) == pl.num_programs(1) - 1)
    def _():
        bufout_ref[...] = xpad_ref[8 + valid_last - 3:8 + valid_last, :]

    xpad_ref[5:8, :] = xpad_ref[8 + c - 3:8 + c, :]


def _rmsnorm_kernel(x_ref, w_ref, o_ref):
    x = x_ref[...]
    ms = jnp.mean(x * x, axis=-1, keepdims=True)
    o_ref[...] = ((x * lax.rsqrt(ms + EPS)) * w_ref[...]).astype(o_ref.dtype)


def _rmsnorm(x, w, out_dtype):
    t, d = x.shape
    tm = min(512, t)
    return pl.pallas_call(
        _rmsnorm_kernel,
        grid=(t // tm,),
        in_specs=[pl.BlockSpec((tm, d), lambda i: (i, 0)),
                  pl.BlockSpec((1, d), lambda i: (0, 0))],
        out_specs=pl.BlockSpec((tm, d), lambda i: (i, 0)),
        out_shape=jax.ShapeDtypeStruct((t, d), out_dtype),
        compiler_params=_params(("arbitrary",)),
        name="rmsnorm",
    )(x, w.reshape(1, d))


def _mm_kernel(*refs, epilogue):
    if epilogue == "residual":
        a_ref, w_ref, r_ref, o_ref, wb_ref = refs
    else:
        a_ref, w_ref, o_ref, wb_ref = refs

    @pl.when(pl.program_id(1) == 0)
    def _():
        wb_ref[...] = w_ref[...].astype(BF16)

    acc = jnp.dot(a_ref[...], wb_ref[...], preferred_element_type=F32)
    if epilogue == "relu2":
        acc = jnp.square(jnp.maximum(acc, 0.0))
    elif epilogue == "residual":
        acc = r_ref[...] + acc
    o_ref[...] = acc.astype(o_ref.dtype)


def _mm_tiles(t, k, n):
    if n % 128 != 0:
        return min(1024, t), n
    tn = {2048: 1024, 4096: 512, 8192: 256}[k]
    tm = {2048: 1024, 4096: 512, 8192: 512}[k]
    return min(tm, t), min(tn, n)


def _matmul(a, w3, layer, col0, n, *, epilogue="none", residual=None, out_dtype=F32):
    t, k = a.shape
    tm, tn = _mm_tiles(t, k, n)
    cb0 = col0 // tn
    in_specs = [pl.BlockSpec((tm, k), lambda j, i: (i, 0)),
                pl.BlockSpec((None, k, tn), lambda j, i: (layer, 0, cb0 + j))]
    args = [a, w3]
    if epilogue == "residual":
        in_specs.append(pl.BlockSpec((tm, tn), lambda j, i: (i, j)))
        args.append(residual)
    return pl.pallas_call(
        functools.partial(_mm_kernel, epilogue=epilogue),
        grid=(n // tn, t // tm),
        in_specs=in_specs,
        out_specs=pl.BlockSpec((tm, tn), lambda j, i: (i, j)),
        out_shape=jax.ShapeDtypeStruct((t, n), out_dtype),
        scratch_shapes=[pltpu.VMEM((k, tn), BF16)],
        compiler_params=_params(("arbitrary", "arbitrary")),
        name="matmul_" + epilogue,
    )(*args)


def _ssd_kernel(z_ref, xbc_ref, dt_ref, dtT_ref, buf_ref, s0_ref, convw_ref, convb_ref,
                dtb_ref, dtbT_ref, alog_ref, alogT_ref, dskip_ref, normw_ref,
                y_ref, s_ref, bufout_ref, xpad_ref, *, c, valid):
    H, P, N, G = SSM_HEADS, SSM_HEAD_DIM, SSM_D_STATE, SSM_GROUPS
    R = H // G

    @pl.when(pl.program_id(1) == 0)
    def _():
        s_ref[...] = s0_ref[...]

    xbc = _silu(_causal_conv(xpad_ref, xbc_ref, buf_ref, convw_ref, c) + convb_ref[...])
    _conv_carry(xpad_ref, bufout_ref, c, valid)

    row = _iota2((c, c), 0)
    col = _iota2((c, c), 1)
    tri = col <= row
    ltri = tri.astype(F32)
    utri = (row <= col).astype(F32)

    dt = _softplus(dt_ref[...] + dtb_ref[...])
    dtT = _softplus(dtT_ref[...] + dtbT_ref[...])
    la = dt * (-jnp.exp(alog_ref[...]))
    laT = dtT * (-jnp.exp(alogT_ref[...]))
    xs = xbc[:, :SSM_D_INNER]
    bm = xbc[:, SSM_D_INNER:SSM_D_INNER + G * N]
    cm = xbc[:, SSM_D_INNER + G * N:]
    if valid < c:
        rmask = _iota2((c, 1), 0) < valid
        cmask = _iota2((1, c), 1) < valid
        la = jnp.where(rmask, la, 0.0)
        laT = jnp.where(cmask, laT, 0.0)
        dt = jnp.where(rmask, dt, 0.0)
        bm = jnp.where(rmask, bm, 0.0)
        cm = jnp.where(rmask, cm, 0.0)
    acum = _dot_f32(ltri, la)
    acumT = _dot_f32(laT, utri)
    alast = acum[c - 1:c, :]

    expand = (jnp.right_shift(_iota2((H, SSM_D_INNER), 1), int(math.log2(P)))
              == _iota2((H, SSM_D_INNER), 0)).astype(F32)
    xdt = xs * _dot_f32(dt, expand)
    ea_x = jnp.exp(_dot_f32(acum, expand))
    xw = xdt * jnp.exp(_dot_f32(alast - acum, expand))
    dskip = _dot_f32(jnp.broadcast_to(dskip_ref[...], (SUBLANES, H)), expand)[0:1, :]

    ys = []
    for g in range(G):
        bg = bm[:, g * N:(g + 1) * N]
        cg = cm[:, g * N:(g + 1) * N]
        cb = _dot_nt(cg, bg)
        sg = s_ref[g * R * P:(g + 1) * R * P, :]
        y_inter = _dot_nt(cg, sg)
        upd = _dot_tn(xw[:, g * R * P:(g + 1) * R * P], bg)
        y_intra = []
        for r in range(R):
            h = g * R + r
            diff = acum[:, h:h + 1] - acumT[h:h + 1, :]
            decay = jnp.where(tri, jnp.exp(jnp.where(tri, diff, 0.0)), 0.0)
            y_intra.append(_dot(cb * decay, xdt[:, h * P:(h + 1) * P]))
            sl = slice(h * P, (h + 1) * P)
            s_ref[sl, :] = jnp.exp(alast[:, h:h + 1]) * s_ref[sl, :] + upd[r * P:(r + 1) * P, :]
        ys.append(jnp.concatenate(y_intra, axis=1) + y_inter * ea_x[:, g * R * P:(g + 1) * R * P])
    y = jnp.concatenate(ys, axis=1) + dskip * xs
    y = y * _silu(z_ref[...])
    gw = SSM_D_INNER // G
    outs = []
    for g in range(G):
        yg = y[:, g * gw:(g + 1) * gw]
        outs.append(yg * lax.rsqrt(jnp.mean(yg * yg, axis=-1, keepdims=True) + EPS))
    y_ref[...] = (jnp.concatenate(outs, axis=1) * normw_ref[...]).astype(y_ref.dtype)


def _ssd(z, xbc, dt, dtT, buf, s0, p, i, *, b, n, c, valid):
    t = b * n * c
    H, P, N = SSM_HEADS, SSM_HEAD_DIM, SSM_D_STATE
    C = SSM_CONV_DIM
    tok = lambda w: pl.BlockSpec((c, w), lambda bb, ii: (bb * n + ii, 0))
    full2 = lambda a: pl.BlockSpec(a.shape, lambda bb, ii: (0, 0))
    convw = p["ssm_conv_w"][i]
    convb = p["ssm_conv_b"][i].reshape(1, C)
    dtb = p["ssm_dt_bias"][i].reshape(1, H)
    dtbT = p["ssm_dt_bias"][i].reshape(H, 1)
    alog = p["ssm_A_log"][i].reshape(1, H)
    alogT = p["ssm_A_log"][i].reshape(H, 1)
    dskip = p["ssm_D"][i].reshape(1, H)
    normw = p["ssm_norm_w"][i].reshape(1, SSM_D_INNER)
    small = [convw, convb, dtb, dtbT, alog, alogT, dskip, normw]
    y, s_new, buf_new = pl.pallas_call(
        functools.partial(_ssd_kernel, c=c, valid=valid),
        grid=(b, n),
        in_specs=[tok(SSM_D_INNER), tok(C), tok(H),
                  pl.BlockSpec((None, None, H, c), lambda bb, ii: (bb, ii, 0, 0)),
                  pl.BlockSpec((None, CONV_W - 1, C), lambda bb, ii: (bb, 0, 0)),
                  pl.BlockSpec((None, H * P, N), lambda bb, ii: (bb, 0, 0))]
                 + [full2(a) for a in small],
        out_specs=[tok(SSM_D_INNER),
                   pl.BlockSpec((None, H * P, N), lambda bb, ii: (bb, 0, 0)),
                   pl.BlockSpec((None, CONV_W - 1, C), lambda bb, ii: (bb, 0, 0))],
        out_shape=[jax.ShapeDtypeStruct((t, SSM_D_INNER), BF16),
                   jax.ShapeDtypeStruct((b, H * P, N), F32),
                   jax.ShapeDtypeStruct((b, CONV_W - 1, C), F32)],
        scratch_shapes=[pltpu.VMEM((8 + c, C), F32)],
        compiler_params=_params(("arbitrary", "arbitrary")),
        name="ssd_scan",
    )(z, xbc, dt, dtT, buf, s0.reshape(b, H * P, N), *small)
    return y, s_new.reshape(b, H, P, N), buf_new


def _gla_kernel(hq_ref, hf_ref, hi_ref, hg_ref, s0_ref, lb_ref, normw_ref,
                o_ref, s_ref, *, c, valid, layer):
    H, K = HG_HEADS, HG_HEAD_DIM

    @pl.when(pl.program_id(1) == 0)
    def _():
        s_ref[...] = s0_ref[...]

    lbp = lb_ref[...]
    mx = lbp[0:1, :]
    for j in range(1, N_AB):
        mx = jnp.maximum(mx, lbp[j:j + 1, :])
    e = [jnp.exp(lbp[j:j + 1, :] - mx) for j in range(N_AB)]
    den = e[0]
    for j in range(1, N_AB):
        den = den + e[j]
    lb = e[0] / den
    for j in range(1, layer + 1):
        lb = lb + e[j] / den
    lb = lb - e[0] / den

    q = _silu(hq_ref[...])
    sig = _sigmoid(hf_ref[...])
    f = lb + (1.0 - lb) * sig
    logf = jnp.log(jnp.maximum(f, TINY))
    k = (1.0 - lb) * (1.0 - sig)
    v = hi_ref[...]
    if valid < c:
        rmask = _iota2((c, 1), 0) < valid
        q = jnp.where(rmask, q, 0.0)
        k = jnp.where(rmask, k, 0.0)
        v = jnp.where(rmask, v, 0.0)
        logf = jnp.where(rmask, logf, 0.0)

    row = _iota2((c, c), 0)
    col = _iota2((c, c), 1)
    ltri = (col <= row).astype(F32)
    g = _dot_f32(ltri, logf)
    glast = g[c - 1:c, :]

    rowv = _iota2((c, 1), 0)
    qs, ks, masks = [q], [k], [row == col]
    m = c // 2
    while m >= 1:
        sh2m = int(math.log2(2 * m))
        blk_r = jnp.right_shift(row, sh2m)
        sel = (col == jnp.left_shift(blk_r, sh2m) + m).astype(F32)
        gref = _dot_f32(sel, g)
        upper = jnp.bitwise_and(rowv, 2 * m - 1) >= m
        qs.append(jnp.where(upper, q * jnp.exp(jnp.where(upper, g - gref, 0.0)), 0.0))
        ks.append(jnp.where(upper, 0.0, k * jnp.exp(jnp.where(upper, 0.0, gref - g))))
        masks.append(blk_r == jnp.right_shift(col, sh2m))
        m //= 2

    qg = q * jnp.exp(g)
    kdec = k * jnp.exp(glast - g)
    glT = jnp.broadcast_to(glast, (SUBLANES, H * K))
    outs = []
    for h in range(H):
        sl = slice(h * K, (h + 1) * K)
        att = jnp.zeros((c, c), F32)
        for ql, kl, ml in zip(qs, ks, masks):
            att = att + jnp.where(ml, _dot_nt(ql[:, sl], kl[:, sl]), 0.0)
        sh = s_ref[sl, :]
        y = _dot(att, v[:, sl]) + _dot(qg[:, sl], sh)
        dec_col = jnp.exp(glT[:, sl].T[:, 0:1])
        s_ref[sl, :] = dec_col * sh + _dot_tn(kdec[:, sl], v[:, sl])
        outs.append(y * lax.rsqrt(jnp.mean(y * y, axis=-1, keepdims=True) + EPS))
    o = jnp.concatenate(outs, axis=1) * normw_ref[...]
    o_ref[...] = (o * _silu(hg_ref[...])).astype(o_ref.dtype)


def _gla(hh, s0, p, i, *, b, n, c, valid):
    t = b * n * c
    H, K = HG_HEADS, HG_HEAD_DIM
    W = HG_WIDTH
    tok = lambda j: pl.BlockSpec((c, W), lambda bb, ii: (bb * n + ii, j))
    full2 = lambda a: pl.BlockSpec(a.shape, lambda bb, ii: (0, 0))
    lb = p["hg_lower_bounds"]
    normw = p["hg_norm_w"][i].reshape(1, W)
    o, s_new = pl.pallas_call(
        functools.partial(_gla_kernel, c=c, valid=valid, layer=i),
        grid=(b, n),
        in_specs=[tok(0), tok(1), tok(2), tok(3),
                  pl.BlockSpec((None, H * K, K), lambda bb, ii: (bb, 0, 0)),
                  full2(lb), full2(normw)],
        out_specs=[pl.BlockSpec((c, W), lambda bb, ii: (bb * n + ii, 0)),
                   pl.BlockSpec((None, H * K, K), lambda bb, ii: (bb, 0, 0))],
        out_shape=[jax.ShapeDtypeStruct((t, W), BF16),
                   jax.ShapeDtypeStruct((b, H * K, K), F32)],
        compiler_params=_params(("arbitrary", "arbitrary")),
        name="gla_scan",
    )(hh, hh, hh, hh, s0.reshape(b, H * K, K), lb, normw)
    return o, s_new.reshape(b, H, K, K)


def _gdn_kernel(qkv_ref, z_ref, ba_ref, baT_ref, buf_ref, s0_ref, convw_ref,
                dtb_ref, dtbT_ref, alog_ref, alogT_ref, normw_ref,
                o_ref, s_ref, bufout_ref, xpad_ref, *, c, valid):
    HV, HK, K, V = GDN_V_HEADS, GDN_K_HEADS, GDN_HEAD_K, GDN_HEAD_V
    rep = HV // HK

    @pl.when(pl.program_id(1) == 0)
    def _():
        s_ref[...] = s0_ref[...]

    qkv = _silu(_causal_conv(xpad_ref, qkv_ref, buf_ref, convw_ref, c))
    _conv_carry(xpad_ref, bufout_ref, c, valid)

    row = _iota2((c, c), 0)
    col = _iota2((c, c), 1)
    tri = col <= row
    strict = col < row
    ltri = tri.astype(F32)
    utri = (row <= col).astype(F32)
    eye = (row == col).astype(F32)

    ba = ba_ref[...]
    baT = baT_ref[...]
    beta = _sigmoid(ba[:, :HV])
    gl = -jnp.exp(alog_ref[...]) * _softplus(ba[:, HV:] + dtb_ref[...])
    glT = -jnp.exp(alogT_ref[...]) * _softplus(baT[HV:, :] + dtbT_ref[...])
    if valid < c:
        rmask = _iota2((c, 1), 0) < valid
        cmask = _iota2((1, c), 1) < valid
        beta = jnp.where(rmask, beta, 0.0)
        gl = jnp.where(rmask, gl, 0.0)
        glT = jnp.where(cmask, glT, 0.0)
        qkv = jnp.where(rmask, qkv, 0.0)
    gcum = _dot_f32(ltri, gl)
    gcumT = _dot_f32(glT, utri)
    egc = jnp.exp(gcum)
    glast = gcum[c - 1:c, :]
    elast = jnp.exp(glast)
    edec = jnp.exp(glast - gcum)

    n_sq = int(math.log2(c)) - 1
    normw = normw_ref[...]
    outs = []
    for hk in range(HK):
        qh = qkv[:, hk * K:(hk + 1) * K]
        kh = qkv[:, GDN_KEY_DIM + hk * K:GDN_KEY_DIM + (hk + 1) * K]
        qh = qh * lax.rsqrt(jnp.sum(qh * qh, axis=-1, keepdims=True) + EPS) * (K ** -0.5)
        kh = kh * lax.rsqrt(jnp.sum(kh * kh, axis=-1, keepdims=True) + EPS)
        qk = _dot_nt(qh, kh)
        for r in range(rep):
            h = hk * rep + r
            vh = qkv[:, 2 * GDN_KEY_DIM + h * V:2 * GDN_KEY_DIM + (h + 1) * V]
            bh = beta[:, h:h + 1]
            diff = gcum[:, h:h + 1] - gcumT[h:h + 1, :]
            gam = jnp.where(tri, jnp.exp(jnp.where(tri, diff, 0.0)), 0.0)
            kb = kh * bh
            x = -jnp.where(strict, _dot_nt(kb, kh) * gam, 0.0)
            tm = eye + x
            pw = x
            for _ in range(n_sq):
                pw = _dot_f32(pw, pw)
                tm = tm + _dot_f32(tm, pw)
            rhs = jnp.concatenate([kb * egc[:, h:h + 1], vh * bh], axis=1)
            wu = _dot(tm, rhs)
            sh = s_ref[h * K:(h + 1) * K, :]
            ws_qs = _dot(jnp.concatenate([wu[:, :K], qh * egc[:, h:h + 1]], axis=0), sh)
            v_new = wu[:, K:] - ws_qs[:c]
            y = ws_qs[c:] + _dot(qk * gam, v_new)
            s_ref[h * K:(h + 1) * K, :] = elast[:, h:h + 1] * sh + _dot_tn(kh * edec[:, h:h + 1], v_new)
            yn = (y * lax.rsqrt(jnp.mean(y * y, axis=-1, keepdims=True) + EPS)) * normw
            outs.append(yn * _silu(z_ref[:, h * V:(h + 1) * V]))
    o_ref[...] = jnp.concatenate(outs, axis=1).astype(o_ref.dtype)


def _gdn(qkv, z, ba, baT, buf, s0, p, i, *, b, n, c, valid):
    t = b * n * c
    HV, K, V = GDN_V_HEADS, GDN_HEAD_K, GDN_HEAD_V
    C = GDN_CONV_DIM
    tok = lambda w: pl.BlockSpec((c, w), lambda bb, ii: (bb * n + ii, 0))
    full2 = lambda a: pl.BlockSpec(a.shape, lambda bb, ii: (0, 0))
    convw = p["gdn_conv_w"][i]
    dtb = p["gdn_dt_bias"][i].reshape(1, HV)
    dtbT = p["gdn_dt_bias"][i].reshape(HV, 1)
    alog = p["gdn_A_log"][i].reshape(1, HV)
    alogT = p["gdn_A_log"][i].reshape(HV, 1)
    normw = p["gdn_norm_w"][i].reshape(1, V)
    small = [convw, dtb, dtbT, alog, alogT, normw]
    o, s_new, buf_new = pl.pallas_call(
        functools.partial(_gdn_kernel, c=c, valid=valid),
        grid=(b, n),
        in_specs=[tok(C), tok(GDN_VAL_DIM), tok(2 * HV),
                  pl.BlockSpec((None, None, 2 * HV, c), lambda bb, ii: (bb, ii, 0, 0)),
                  pl.BlockSpec((None, CONV_W - 1, C), lambda bb, ii: (bb, 0, 0)),
                  pl.BlockSpec((None, HV * K, V), lambda bb, ii: (bb, 0, 0))]
                 + [full2(a) for a in small],
        out_specs=[tok(GDN_VAL_DIM),
                   pl.BlockSpec((None, HV * K, V), lambda bb, ii: (bb, 0, 0)),
                   pl.BlockSpec((None, CONV_W - 1, C), lambda bb, ii: (bb, 0, 0))],
        out_shape=[jax.ShapeDtypeStruct((t, GDN_VAL_DIM), BF16),
                   jax.ShapeDtypeStruct((b, HV * K, V), F32),
                   jax.ShapeDtypeStruct((b, CONV_W - 1, C), F32)],
        scratch_shapes=[pltpu.VMEM((8 + c, C), F32)],
        compiler_params=_params(("arbitrary", "arbitrary")),
        name="gdn_scan",
    )(qkv, z, ba, baT, buf, s0.reshape(b, HV * K, V), *small)
    return o, s_new.reshape(b, HV, K, V), buf_new


def _chunk_transposed(a, b, n, c):
    return jnp.swapaxes(a.reshape(b, n, c, a.shape[-1]), 2, 3)


def _trunk(x, st_ssm, st_ssm_conv, st_hg, st_gdn, st_gdn_conv, p, *, c, valid):
    b, lp, d = x.shape
    n = lp // c
    t = b * lp
    xt = x.reshape(t, d)
    dims = dict(b=b, n=n, c=c, valid=valid)
    n_ssm, n_ssm_conv, n_hg, n_gdn, n_gdn_conv = [], [], [], [], []
    for l in range(DEPTH):
        i = l // 2
        h = _rmsnorm(xt, p["norm_mix"][l], BF16)
        if l % 2 == 0:
            z = _matmul(h, p["w_in_ab"], i, 0, SSM_D_INNER)
            xbc = _matmul(h, p["w_in_ab"], i, AB_OFF_XBC, SSM_CONV_DIM)
            dt = _matmul(h, p["w_in_dt"], i, 0, SSM_HEADS)
            hh = _matmul(h, p["w_in_hg"], i, 0, 4 * HG_WIDTH)
            y, s1, b1 = _ssd(z, xbc, dt, _chunk_transposed(dt, b, n, c), st_ssm_conv[i], st_ssm[i],
                             p, i, **dims)
            o, s2 = _gla(hh, st_hg[i], p, i, **dims)
            n_ssm.append(s1); n_ssm_conv.append(b1); n_hg.append(s2)
            mix = jnp.concatenate([y, o], axis=-1)
            xt = _matmul(mix, p["w_out_ab"], i, 0, D_MODEL, epilogue="residual", residual=xt)
        else:
            qkv = _matmul(h, p["w_in_c"], i, 0, GDN_CONV_DIM)
            z = _matmul(h, p["w_in_c"], i, C_OFF_Z, GDN_VAL_DIM)
            ba = _matmul(h, p["w_in_ba"], i, 0, 2 * GDN_V_HEADS)
            o, s3, b3 = _gdn(qkv, z, ba, _chunk_transposed(ba, b, n, c), st_gdn_conv[i], st_gdn[i],
                             p, i, **dims)
            n_gdn.append(s3); n_gdn_conv.append(b3)
            xt = _matmul(o, p["w_out_c"], i, 0, D_MODEL, epilogue="residual", residual=xt)
        h = _rmsnorm(xt, p["norm_mlp"][l], BF16)
        u = _matmul(h, p["w_ff1"], l, 0, FFN_HIDDEN, epilogue="relu2", out_dtype=BF16)
        xt = _matmul(u, p["w_ff2"], l, 0, D_MODEL, epilogue="residual", residual=xt)
    y = _rmsnorm(xt, p["norm_final"], F32).reshape(b, lp, d)
    return (y, jnp.stack(n_ssm), jnp.stack(n_ssm_conv), jnp.stack(n_hg), jnp.stack(n_gdn),
            jnp.stack(n_gdn_conv))


def kernel(x_prompt, x_sample, state_ssm, state_ssm_conv, state_hgrn, state_gdn, state_gdn_conv, norm_mix, norm_mlp, norm_final, w_in_ab, ssm_conv_w, ssm_conv_b, ssm_dt_bias, ssm_A_log, ssm_D, ssm_norm_w, hg_lower_bounds, hg_norm_w, w_out_ab, w_in_c, gdn_conv_w, gdn_dt_bias, gdn_A_log, gdn_norm_w, w_out_c, w_ff1, w_ff2):
    p = dict(norm_mix=norm_mix, norm_mlp=norm_mlp, norm_final=norm_final, w_in_ab=w_in_ab,
             ssm_conv_w=ssm_conv_w, ssm_conv_b=ssm_conv_b, ssm_dt_bias=ssm_dt_bias, ssm_A_log=ssm_A_log,
             ssm_D=ssm_D, ssm_norm_w=ssm_norm_w, hg_lower_bounds=hg_lower_bounds, hg_norm_w=hg_norm_w,
             w_out_ab=w_out_ab, w_in_c=w_in_c, gdn_conv_w=gdn_conv_w, gdn_dt_bias=gdn_dt_bias,
             gdn_A_log=gdn_A_log, gdn_norm_w=gdn_norm_w, w_out_c=w_out_c, w_ff1=w_ff1, w_ff2=w_ff2)
    p["w_in_dt"] = w_in_ab[:, :, AB_OFF_DT:AB_OFF_HG]
    p["w_in_hg"] = w_in_ab[:, :, AB_OFF_HG:]
    p["w_in_ba"] = w_in_c[:, :, C_OFF_BA:]

    bp, seq, _ = x_prompt.shape
    z_ssm = jnp.zeros((N_AB, bp, SSM_HEADS, SSM_HEAD_DIM, SSM_D_STATE), F32)
    z_ssm_conv = jnp.zeros((N_AB, bp, CONV_W - 1, SSM_CONV_DIM), F32)
    z_hg = jnp.zeros((N_AB, bp, HG_HEADS, HG_HEAD_DIM, HG_HEAD_DIM), F32)
    z_gdn = jnp.zeros((N_C, bp, GDN_V_HEADS, GDN_HEAD_K, GDN_HEAD_V), F32)
    z_gdn_conv = jnp.zeros((N_C, bp, CONV_W - 1, GDN_CONV_DIM), F32)
    c_p = min(CHUNK, seq)
    outs_p = _trunk(x_prompt, z_ssm, z_ssm_conv, z_hg, z_gdn, z_gdn_conv, p, c=c_p, valid=c_p)

    bs, dec, _ = x_sample.shape
    c_s = -(-dec // SUBLANES) * SUBLANES
    xs_pad = jnp.pad(x_sample, ((0, 0), (0, c_s - dec), (0, 0)))
    outs_s = _trunk(xs_pad, state_ssm, state_ssm_conv, state_hgrn, state_gdn, state_gdn_conv, p,
                    c=c_s, valid=dec)
    y_sample = outs_s[0][:, :dec]
    return (outs_p[0], y_sample) + tuple(outs_p[1:]) + tuple(outs_s[1:])
```

```python
import functools
import math

import jax
import jax.numpy as jnp
from jax import lax
from jax.experimental import pallas as pl
from jax.experimental.pallas import tpu as pltpu

F32 = jnp.float32
BF16 = jnp.bfloat16
HIGHEST = lax.Precision.HIGHEST

D_MODEL = 2048
DEPTH = 4
N_AB = (DEPTH + 1) // 2
N_C = DEPTH // 2
CONV_W = 4
CHUNK = 64
EPS = 1e-6
TINY = 1e-30

SSM_D_INNER = D_MODEL
SSM_HEAD_DIM = 64
SSM_HEADS = SSM_D_INNER // SSM_HEAD_DIM
SSM_GROUPS = 4
SSM_D_STATE = 128
SSM_CONV_DIM = SSM_D_INNER + 2 * SSM_GROUPS * SSM_D_STATE

HG_WIDTH = D_MODEL
HG_HEAD_DIM = 128
HG_HEADS = HG_WIDTH // HG_HEAD_DIM

GDN_HEAD_K = 128
GDN_HEAD_V = 128
GDN_K_HEADS = D_MODEL // GDN_HEAD_K
GDN_V_HEADS = 2 * GDN_K_HEADS
GDN_KEY_DIM = GDN_K_HEADS * GDN_HEAD_K
GDN_VAL_DIM = GDN_V_HEADS * GDN_HEAD_V
GDN_CONV_DIM = 2 * GDN_KEY_DIM + GDN_VAL_DIM

FFN_HIDDEN = 4 * D_MODEL

AB_OFF_XBC = SSM_D_INNER
AB_OFF_DT = AB_OFF_XBC + SSM_CONV_DIM
AB_OFF_HG = AB_OFF_DT + SSM_HEADS
C_OFF_Z = GDN_CONV_DIM
C_OFF_BA = C_OFF_Z + GDN_VAL_DIM

SUBLANES = 8
VMEM_LIMIT = 52 * 1024 * 1024
GDN_WAVE = 8


def _params(sem):
    return pltpu.CompilerParams(dimension_semantics=sem, vmem_limit_bytes=VMEM_LIMIT)


def _dot(a, b):
    return jnp.dot(a.astype(BF16), b.astype(BF16), preferred_element_type=F32)


def _dot_nt(a, b):
    return lax.dot_general(a.astype(BF16), b.astype(BF16), (((1,), (1,)), ((), ())),
                           preferred_element_type=F32)


def _dot_tn(a, b):
    return lax.dot_general(a.astype(BF16), b.astype(BF16), (((0,), (0,)), ((), ())),
                           preferred_element_type=F32)


def _dot_f32(a, b):
    return jnp.dot(a, b, precision=HIGHEST, preferred_element_type=F32)


def _sigmoid(x):
    return 1.0 / (1.0 + jnp.exp(-x))


def _silu(x):
    return x * _sigmoid(x)


def _softplus(x):
    return jnp.maximum(x, 0.0) + jnp.log1p(jnp.exp(-jnp.abs(x)))


def _iota2(shape, dim):
    return lax.broadcasted_iota(jnp.int32, shape, dim)


def _causal_conv(xpad_ref, x_ref, buf_ref, w_ref, c):
    @pl.when(pl.program_id(1) == 0)
    def _():
        xpad_ref[5:8, :] = buf_ref[...]

    xpad_ref[8:8 + c, :] = x_ref[...]
    w = w_ref[...]
    y = xpad_ref[5:5 + c, :] * w[0:1, :]
    for j in range(1, CONV_W):
        y = y + xpad_ref[5 + j:5 + j + c, :] * w[j:j + 1, :]
    return y


def _conv_carry(xpad_ref, bufout_ref, c, valid_last):
    @pl.when(pl.program_id(1) == pl.num_programs(1) - 1)
    def _():
        bufout_ref[...] = xpad_ref[8 + valid_last - 3:8 + valid_last, :]

    xpad_ref[5:8, :] = xpad_ref[8 + c - 3:8 + c, :]


def _rmsnorm_kernel(x_ref, w_ref, o_ref):
    x = x_ref[...]
    ms = jnp.mean(x * x, axis=-1, keepdims=True)
    o_ref[...] = ((x * lax.rsqrt(ms + EPS)) * w_ref[...]).astype(o_ref.dtype)


def _rmsnorm(x, w, out_dtype):
    t, d = x.shape
    tm = min(512, t)
    return pl.pallas_call(
        _rmsnorm_kernel,
        grid=(t // tm,),
        in_specs=[pl.BlockSpec((tm, d), lambda i: (i, 0)),
                  pl.BlockSpec((1, d), lambda i: (0, 0))],
        out_specs=pl.BlockSpec((tm, d), lambda i: (i, 0)),
        out_shape=jax.ShapeDtypeStruct((t, d), out_dtype),
        compiler_params=_params(("arbitrary",)),
        name="rmsnorm",
    )(x, w.reshape(1, d))


def _mm_kernel(*refs, epilogue):
    if epilogue == "residual":
        a_ref, w_ref, r_ref, o_ref, wb_ref = refs
    else:
        a_ref, w_ref, o_ref, wb_ref = refs

    @pl.when(pl.program_id(1) == 0)
    def _():
        wb_ref[...] = w_ref[...].astype(BF16)

    acc = jnp.dot(a_ref[...], wb_ref[...], preferred_element_type=F32)
    if epilogue == "relu2":
        acc = jnp.square(jnp.maximum(acc, 0.0))
    elif epilogue == "residual":
        acc = r_ref[...] + acc
    o_ref[...] = acc.astype(o_ref.dtype)


def _mm_tiles(t, k, n):
    if n % 128 != 0:
        return min(1024, t), n, 2
    tn = {2048: 1024, 4096: 512, 8192: 512}[k]
    tm = {2048: 1024, 4096: 512, 8192: 512}[k]
    return min(tm, t), min(tn, n), (1 if k == 8192 else 2)


def _matmul(a, w3, layer, col0, n, *, epilogue="none", residual=None, out_dtype=F32):
    t, k = a.shape
    tm, tn, w_bufs = _mm_tiles(t, k, n)
    cb0 = col0 // tn
    in_specs = [pl.BlockSpec((tm, k), lambda j, i: (i, 0)),
                pl.BlockSpec((None, k, tn), lambda j, i: (layer, 0, cb0 + j),
                             pipeline_mode=pl.Buffered(w_bufs))]
    args = [a, w3]
    if epilogue == "residual":
        in_specs.append(pl.BlockSpec((tm, tn), lambda j, i: (i, j)))
        args.append(residual)
    return pl.pallas_call(
        functools.partial(_mm_kernel, epilogue=epilogue),
        grid=(n // tn, t // tm),
        in_specs=in_specs,
        out_specs=pl.BlockSpec((tm, tn), lambda j, i: (i, j)),
        out_shape=jax.ShapeDtypeStruct((t, n), out_dtype),
        scratch_shapes=[pltpu.VMEM((k, tn), BF16)],
        compiler_params=_params(("arbitrary", "arbitrary")),
        name="matmul_" + epilogue,
    )(*args)


def _ssd_kernel(z_ref, xbc_ref, dt_ref, dtT_ref, buf_ref, s0_ref, convw_ref, convb_ref,
                dtb_ref, dtbT_ref, alog_ref, alogT_ref, dskip_ref, normw_ref,
                y_ref, s_ref, bufout_ref, xpad_ref, *, c, valid):
    H, P, N, G = SSM_HEADS, SSM_HEAD_DIM, SSM_D_STATE, SSM_GROUPS
    R = H // G

    @pl.when(pl.program_id(1) == 0)
    def _():
        s_ref[...] = s0_ref[...]

    xbc = _silu(_causal_conv(xpad_ref, xbc_ref, buf_ref, convw_ref, c) + convb_ref[...])
    _conv_carry(xpad_ref, bufout_ref, c, valid)

    row = _iota2((c, c), 0)
    col = _iota2((c, c), 1)
    tri = col <= row
    ltri = tri.astype(F32)
    utri = (row <= col).astype(F32)

    dt = _softplus(dt_ref[...] + dtb_ref[...])
    dtT = _softplus(dtT_ref[...] + dtbT_ref[...])
    la = dt * (-jnp.exp(alog_ref[...]))
    laT = dtT * (-jnp.exp(alogT_ref[...]))
    xs = xbc[:, :SSM_D_INNER]
    bm = xbc[:, SSM_D_INNER:SSM_D_INNER + G * N]
    cm = xbc[:, SSM_D_INNER + G * N:]
    if valid < c:
        rmask = _iota2((c, 1), 0) < valid
        cmask = _iota2((1, c), 1) < valid
        la = jnp.where(rmask, la, 0.0)
        laT = jnp.where(cmask, laT, 0.0)
        dt = jnp.where(rmask, dt, 0.0)
        bm = jnp.where(rmask, bm, 0.0)
        cm = jnp.where(rmask, cm, 0.0)
    acum = _dot_f32(ltri, la)
    acumT = _dot_f32(laT, utri)
    alast = acum[c - 1:c, :]

    expand = (jnp.right_shift(_iota2((H, SSM_D_INNER), 1), int(math.log2(P)))
              == _iota2((H, SSM_D_INNER), 0)).astype(F32)
    xdt = xs * _dot_f32(dt, expand)
    ea_x = jnp.exp(_dot_f32(acum, expand))
    xw = xdt * jnp.exp(_dot_f32(alast - acum, expand))
    dskip = _dot_f32(jnp.broadcast_to(dskip_ref[...], (SUBLANES, H)), expand)[0:1, :]

    ys = []
    for g in range(G):
        bg = bm[:, g * N:(g + 1) * N]
        cg = cm[:, g * N:(g + 1) * N]
        cb = _dot_nt(cg, bg)
        sg = s_ref[g * R * P:(g + 1) * R * P, :]
        y_inter = _dot_nt(cg, sg)
        upd = _dot_tn(xw[:, g * R * P:(g + 1) * R * P], bg)
        y_intra = []
        for r in range(R):
            h = g * R + r
            diff = acum[:, h:h + 1] - acumT[h:h + 1, :]
            decay = jnp.where(tri, jnp.exp(jnp.where(tri, diff, 0.0)), 0.0)
            y_intra.append(_dot(cb * decay, xdt[:, h * P:(h + 1) * P]))
            sl = slice(h * P, (h + 1) * P)
            s_ref[sl, :] = jnp.exp(alast[:, h:h + 1]) * s_ref[sl, :] + upd[r * P:(r + 1) * P, :]
        ys.append(jnp.concatenate(y_intra, axis=1) + y_inter * ea_x[:, g * R * P:(g + 1) * R * P])
    y = jnp.concatenate(ys, axis=1) + dskip * xs
    y = y * _silu(z_ref[...])
    gw = SSM_D_INNER // G
    outs = []
    for g in range(G):
        yg = y[:, g * gw:(g + 1) * gw]
        outs.append(yg * lax.rsqrt(jnp.mean(yg * yg, axis=-1, keepdims=True) + EPS))
    y_ref[...] = (jnp.concatenate(outs, axis=1) * normw_ref[...]).astype(y_ref.dtype)


def _layer_state_io(s_all, acc, layer, rows, cols):
    n_layers, b = s_all.shape[:2]
    spec = pl.BlockSpec((None, None, rows, cols), lambda bb, ii: (layer, bb, 0, 0))
    shape = jax.ShapeDtypeStruct((n_layers, b, rows, cols), F32)
    extra_args, extra_specs = ([], []) if acc is None else ([acc], [pl.BlockSpec(memory_space=pl.ANY)])
    return s_all.reshape(n_layers, b, rows, cols), spec, shape, extra_args, extra_specs


def _without_ref(kern, pos, present):
    if not present:
        return kern
    return lambda *refs: kern(*refs[:pos], *refs[pos + 1:])


def _ssd(z, xbc, dt, dtT, buf, s_all, s_acc, p, i, *, b, n, c, valid):
    t = b * n * c
    H, P, N = SSM_HEADS, SSM_HEAD_DIM, SSM_D_STATE
    C = SSM_CONV_DIM
    tok = lambda w: pl.BlockSpec((c, w), lambda bb, ii: (bb * n + ii, 0))
    full2 = lambda a: pl.BlockSpec(a.shape, lambda bb, ii: (0, 0))
    convw = p["ssm_conv_w"][i]
    convb = p["ssm_conv_b"][i].reshape(1, C)
    dtb = p["ssm_dt_bias"][i].reshape(1, H)
    dtbT = p["ssm_dt_bias"][i].reshape(H, 1)
    alog = p["ssm_A_log"][i].reshape(1, H)
    alogT = p["ssm_A_log"][i].reshape(H, 1)
    dskip = p["ssm_D"][i].reshape(1, H)
    normw = p["ssm_norm_w"][i].reshape(1, SSM_D_INNER)
    small = [convw, convb, dtb, dtbT, alog, alogT, dskip, normw]
    s_in, s_spec, s_shape, acc_args, acc_specs = _layer_state_io(s_all, s_acc, i, H * P, N)
    n_in = 6 + len(small)
    y, s_new, buf_new = pl.pallas_call(
        _without_ref(functools.partial(_ssd_kernel, c=c, valid=valid), n_in, bool(acc_args)),
        grid=(b, n),
        in_specs=[tok(SSM_D_INNER), tok(C), tok(H),
                  pl.BlockSpec((None, None, H, c), lambda bb, ii: (bb, ii, 0, 0)),
                  pl.BlockSpec((None, CONV_W - 1, C), lambda bb, ii: (bb, 0, 0)),
                  s_spec]
                 + [full2(a) for a in small] + acc_specs,
        out_specs=[tok(SSM_D_INNER), s_spec,
                   pl.BlockSpec((None, CONV_W - 1, C), lambda bb, ii: (bb, 0, 0))],
        out_shape=[jax.ShapeDtypeStruct((t, SSM_D_INNER), BF16), s_shape,
                   jax.ShapeDtypeStruct((b, CONV_W - 1, C), F32)],
        scratch_shapes=[pltpu.VMEM((8 + c, C), F32)],
        input_output_aliases={n_in: 1} if acc_args else {},
        compiler_params=_params(("arbitrary", "arbitrary")),
        name="ssd_scan",
    )(z, xbc, dt, dtT, buf, s_in, *small, *acc_args)
    return y, s_new, buf_new


def _gla_kernel(hq_ref, hf_ref, hi_ref, hg_ref, s0_ref, lb_ref, normw_ref,
                o_ref, s_ref, *, c, valid, layer):
    H, K = HG_HEADS, HG_HEAD_DIM

    @pl.when(pl.program_id(1) == 0)
    def _():
        s_ref[...] = s0_ref[...]

    lbp = lb_ref[...]
    mx = lbp[0:1, :]
    for j in range(1, N_AB):
        mx = jnp.maximum(mx, lbp[j:j + 1, :])
    e = [jnp.exp(lbp[j:j + 1, :] - mx) for j in range(N_AB)]
    den = e[0]
    for j in range(1, N_AB):
        den = den + e[j]
    lb = e[0] / den
    for j in range(1, layer + 1):
        lb = lb + e[j] / den
    lb = lb - e[0] / den

    q = _silu(hq_ref[...])
    sig = _sigmoid(hf_ref[...])
    f = lb + (1.0 - lb) * sig
    logf = jnp.log(jnp.maximum(f, TINY))
    k = (1.0 - lb) * (1.0 - sig)
    v = hi_ref[...]
    if valid < c:
        rmask = _iota2((c, 1), 0) < valid
        q = jnp.where(rmask, q, 0.0)
        k = jnp.where(rmask, k, 0.0)
        v = jnp.where(rmask, v, 0.0)
        logf = jnp.where(rmask, logf, 0.0)

    row = _iota2((c, c), 0)
    col = _iota2((c, c), 1)
    ltri = (col <= row).astype(F32)
    g = _dot_f32(ltri, logf)
    glast = g[c - 1:c, :]

    rowv = _iota2((c, 1), 0)
    qs, ks, masks = [q], [k], [row == col]
    m = c // 2
    while m >= 1:
        sh2m = int(math.log2(2 * m))
        blk_r = jnp.right_shift(row, sh2m)
        sel = (col == jnp.left_shift(blk_r, sh2m) + m).astype(F32)
        gref = _dot_f32(sel, g)
        upper = jnp.bitwise_and(rowv, 2 * m - 1) >= m
        qs.append(jnp.where(upper, q * jnp.exp(jnp.where(upper, g - gref, 0.0)), 0.0))
        ks.append(jnp.where(upper, 0.0, k * jnp.exp(jnp.where(upper, 0.0, gref - g))))
        masks.append(blk_r == jnp.right_shift(col, sh2m))
        m //= 2

    qg = q * jnp.exp(g)
    kdec = k * jnp.exp(glast - g)
    glT = jnp.broadcast_to(glast, (SUBLANES, H * K))
    outs = []
    for h in range(H):
        sl = slice(h * K, (h + 1) * K)
        att = jnp.zeros((c, c), F32)
        for ql, kl, ml in zip(qs, ks, masks):
            att = att + jnp.where(ml, _dot_nt(ql[:, sl], kl[:, sl]), 0.0)
        sh = s_ref[sl, :]
        y = _dot(att, v[:, sl]) + _dot(qg[:, sl], sh)
        dec_col = jnp.exp(glT[:, sl].T[:, 0:1])
        s_ref[sl, :] = dec_col * sh + _dot_tn(kdec[:, sl], v[:, sl])
        outs.append(y * lax.rsqrt(jnp.mean(y * y, axis=-1, keepdims=True) + EPS))
    o = jnp.concatenate(outs, axis=1) * normw_ref[...]
    o_ref[...] = (o * _silu(hg_ref[...])).astype(o_ref.dtype)


def _gla(hh, s_all, s_acc, p, i, *, b, n, c, valid):
    t = b * n * c
    H, K = HG_HEADS, HG_HEAD_DIM
    W = HG_WIDTH
    tok = lambda j: pl.BlockSpec((c, W), lambda bb, ii: (bb * n + ii, j))
    full2 = lambda a: pl.BlockSpec(a.shape, lambda bb, ii: (0, 0))
    lb = p["hg_lower_bounds"]
    normw = p["hg_norm_w"][i].reshape(1, W)
    s_in, s_spec, s_shape, acc_args, acc_specs = _layer_state_io(s_all, s_acc, i, H * K, K)
    n_in = 7
    o, s_new = pl.pallas_call(
        _without_ref(functools.partial(_gla_kernel, c=c, valid=valid, layer=i), n_in, bool(acc_args)),
        grid=(b, n),
        in_specs=[tok(0), tok(1), tok(2), tok(3), s_spec, full2(lb), full2(normw)] + acc_specs,
        out_specs=[pl.BlockSpec((c, W), lambda bb, ii: (bb * n + ii, 0)), s_spec],
        out_shape=[jax.ShapeDtypeStruct((t, W), BF16), s_shape],
        input_output_aliases={n_in: 1} if acc_args else {},
        compiler_params=_params(("arbitrary", "arbitrary")),
        name="gla_scan",
    )(hh, hh, hh, hh, s_in, lb, normw, *acc_args)
    return o, s_new


def _split_bf16(x):
    hi = x.astype(BF16)
    return hi, (x - hi.astype(F32)).astype(BF16)


def _gdn_kernel(qkv_ref, z_ref, ba_ref, aT2_ref, buf_ref, s0_ref, convw_ref,
                dtb_ref, dtb2_ref, alog_ref, alog2_ref, normw_ref,
                o_ref, s_ref, bufout_ref, xpad_ref, *, c, valid, wave):
    HV, HK, K, V = GDN_V_HEADS, GDN_K_HEADS, GDN_HEAD_K, GDN_HEAD_V
    L = 2 * c
    log2c = int(math.log2(c))

    @pl.when(pl.program_id(1) == 0)
    def _():
        s_ref[...] = s0_ref[...]

    qkv = _silu(_causal_conv(xpad_ref, qkv_ref, buf_ref, convw_ref, c))
    _conv_carry(xpad_ref, bufout_ref, c, valid)

    row = _iota2((c, c), 0)
    col = _iota2((c, c), 1)
    ltri = (col <= row).astype(F32)
    rowp = _iota2((c, L), 0)
    lanep = _iota2((c, L), 1)
    left = lanep < c
    sp = jnp.bitwise_and(lanep, c - 1)
    tri2 = sp <= rowp
    strict2 = sp < rowp
    eye2 = (sp == rowp).astype(F32)
    rl = _iota2((L, L), 0)
    cl = _iota2((L, L), 1)
    utri2 = jnp.where(jnp.right_shift(rl, log2c) == jnp.right_shift(cl, log2c),
                      (rl <= cl).astype(F32), 0.0)

    ba = ba_ref[...]
    beta = _sigmoid(ba[:, :HV])
    gl = -jnp.exp(alog_ref[...]) * _softplus(ba[:, HV:] + dtb_ref[...])
    glT2 = -jnp.exp(alog2_ref[...]) * _softplus(aT2_ref[...] + dtb2_ref[...])
    if valid < c:
        rmask = _iota2((c, 1), 0) < valid
        cmask = jnp.bitwise_and(_iota2((1, L), 1), c - 1) < valid
        beta = jnp.where(rmask, beta, 0.0)
        gl = jnp.where(rmask, gl, 0.0)
        glT2 = jnp.where(cmask, glT2, 0.0)
        qkv = jnp.where(rmask, qkv, 0.0)
    gcum = _dot_f32(ltri, gl)
    gcumT2 = _dot_f32(glT2, utri2)
    egc = jnp.exp(gcum)
    glast = gcum[c - 1:c, :]
    elast = jnp.exp(glast)
    edec = jnp.exp(glast - gcum)

    n_sq = log2c - 1
    normw = normw_ref[...]
    zeros_cv = jnp.zeros((c, V), F32)

    def pair_cols(a, hk):
        return jnp.where(left, a[:, 2 * hk:2 * hk + 1], a[:, 2 * hk + 1:2 * hk + 2])

    def block_diag(x2):
        return jnp.concatenate([jnp.where(left, x2, 0.0), jnp.where(left, 0.0, x2)], axis=0)

    def pair_product(lhs_split, x2):
        lh, ll = lhs_split
        rh, rlo = _split_bf16(block_diag(x2))
        return (jnp.dot(lh, rh, preferred_element_type=F32)
                + jnp.dot(ll, rh, preferred_element_type=F32)
                + jnp.dot(lh, rlo, preferred_element_type=F32))

    outs = [None] * HV
    for w0 in range(0, HK, wave):
        hks = list(range(w0, min(w0 + wave, HK)))
        qn, kn, gam, qkg, pw, tm = {}, {}, {}, {}, {}, {}
        for hk in hks:
            qh = qkv[:, hk * K:(hk + 1) * K]
            kh = qkv[:, GDN_KEY_DIM + hk * K:GDN_KEY_DIM + (hk + 1) * K]
            qn[hk] = qh * lax.rsqrt(jnp.sum(qh * qh, axis=-1, keepdims=True) + EPS) * (K ** -0.5)
            kn[hk] = kh * lax.rsqrt(jnp.sum(kh * kh, axis=-1, keepdims=True) + EPS)
            gq = _dot_nt(jnp.concatenate([kn[hk], qn[hk]], axis=0),
                         jnp.concatenate([kn[hk], kn[hk]], axis=0))
            diff = pair_cols(gcum, hk) - gcumT2[hk:hk + 1, :]
            gam[hk] = jnp.where(tri2, jnp.exp(jnp.where(tri2, diff, 0.0)), 0.0)
            pw[hk] = -jnp.where(strict2, pair_cols(beta, hk) * gq[:c] * gam[hk], 0.0)
            qkg[hk] = gq[c:] * gam[hk]
            tm[hk] = eye2 + pw[hk]
        for _ in range(n_sq):
            for hk in hks:
                pw[hk] = pair_product(_split_bf16(pw[hk]), pw[hk])
            for hk in hks:
                tm[hk] = tm[hk] + pair_product(_split_bf16(tm[hk]), pw[hk])
        wu, ws_qs, v_new = {}, {}, {}
        for hk in hks:
            rhs = []
            for j in range(2):
                h = 2 * hk + j
                bh = beta[:, h:h + 1]
                vh = qkv[:, 2 * GDN_KEY_DIM + h * V:2 * GDN_KEY_DIM + (h + 1) * V]
                rhs.append(jnp.concatenate([kn[hk] * (bh * egc[:, h:h + 1]), vh * bh], axis=1))
            wu[hk] = _dot(block_diag(tm[hk]), jnp.concatenate(rhs, axis=0))
        for hk in hks:
            for j in range(2):
                h = 2 * hk + j
                wq = jnp.concatenate([wu[hk][j * c:(j + 1) * c, :K], qn[hk] * egc[:, h:h + 1]], axis=0)
                ws_qs[h] = _dot(wq, s_ref[h * K:(h + 1) * K, :])
                v_new[h] = wu[hk][j * c:(j + 1) * c, K:] - ws_qs[h][:c]
        for hk in hks:
            ha, hb = 2 * hk, 2 * hk + 1
            y2 = _dot(block_diag(qkg[hk]), jnp.concatenate([v_new[ha], v_new[hb]], axis=0))
            kdec = jnp.concatenate([kn[hk] * edec[:, ha:ha + 1], kn[hk] * edec[:, hb:hb + 1]], axis=0)
            vbd = jnp.concatenate([jnp.concatenate([v_new[ha], zeros_cv], axis=1),
                                   jnp.concatenate([zeros_cv, v_new[hb]], axis=1)], axis=0)
            upd = _dot_tn(kdec, vbd)
            for j, h in enumerate((ha, hb)):
                sl = slice(h * K, (h + 1) * K)
                s_ref[sl, :] = elast[:, h:h + 1] * s_ref[sl, :] + upd[:, j * V:(j + 1) * V]
                y = ws_qs[h][c:] + y2[j * c:(j + 1) * c]
                yn = (y * lax.rsqrt(jnp.mean(y * y, axis=-1, keepdims=True) + EPS)) * normw
                outs[h] = yn * _silu(z_ref[:, h * V:(h + 1) * V])
    o_ref[...] = jnp.concatenate(outs, axis=1).astype(o_ref.dtype)


def _pair_packed_transpose(a, b, n, c):
    hh = a.shape[-1] // 2
    x = a.reshape(b, n, c, hh, 2)
    return jnp.transpose(x, (0, 1, 3, 4, 2)).reshape(b, n, hh, 2 * c)


def _gdn(qkv, z, ba, buf, s_all, s_acc, p, i, *, b, n, c, valid):
    t = b * n * c
    HV, HK, K, V = GDN_V_HEADS, GDN_K_HEADS, GDN_HEAD_K, GDN_HEAD_V
    C = GDN_CONV_DIM
    tok = lambda w: pl.BlockSpec((c, w), lambda bb, ii: (bb * n + ii, 0))
    full2 = lambda a: pl.BlockSpec(a.shape, lambda bb, ii: (0, 0))
    aT2 = _pair_packed_transpose(ba[:, HV:], b, n, c)
    convw = p["gdn_conv_w"][i]
    dtb = p["gdn_dt_bias"][i].reshape(1, HV)
    dtb2 = jnp.repeat(p["gdn_dt_bias"][i].reshape(HK, 2), c, axis=1)
    alog = p["gdn_A_log"][i].reshape(1, HV)
    alog2 = jnp.repeat(p["gdn_A_log"][i].reshape(HK, 2), c, axis=1)
    normw = p["gdn_norm_w"][i].reshape(1, V)
    small = [convw, dtb, dtb2, alog, alog2, normw]
    s_in, s_spec, s_shape, acc_args, acc_specs = _layer_state_io(s_all, s_acc, i, HV * K, V)
    n_in = 6 + len(small)
    o, s_new, buf_new = pl.pallas_call(
        _without_ref(functools.partial(_gdn_kernel, c=c, valid=valid, wave=GDN_WAVE), n_in, bool(acc_args)),
        grid=(b, n),
        in_specs=[tok(C), tok(GDN_VAL_DIM), tok(2 * HV),
                  pl.BlockSpec((None, None, HK, 2 * c), lambda bb, ii: (bb, ii, 0, 0)),
                  pl.BlockSpec((None, CONV_W - 1, C), lambda bb, ii: (bb, 0, 0)),
                  s_spec]
                 + [full2(a) for a in small] + acc_specs,
        out_specs=[tok(GDN_VAL_DIM), s_spec,
                   pl.BlockSpec((None, CONV_W - 1, C), lambda bb, ii: (bb, 0, 0))],
        out_shape=[jax.ShapeDtypeStruct((t, GDN_VAL_DIM), BF16), s_shape,
                   jax.ShapeDtypeStruct((b, CONV_W - 1, C), F32)],
        scratch_shapes=[pltpu.VMEM((8 + c, C), F32)],
        input_output_aliases={n_in: 1} if acc_args else {},
        compiler_params=_params(("arbitrary", "arbitrary")),
        name="gdn_scan",
    )(qkv, z, ba, aT2, buf, s_in, *small, *acc_args)
    return o, s_new, buf_new


def _chunk_transposed(a, b, n, c):
    return jnp.swapaxes(a.reshape(b, n, c, a.shape[-1]), 2, 3)


def _trunk(x, st_ssm, st_ssm_conv, st_hg, st_gdn, st_gdn_conv, p, *, c, valid):
    b, lp, d = x.shape
    n = lp // c
    t = b * lp
    xt = x.reshape(t, d)
    dims = dict(b=b, n=n, c=c, valid=valid)
    n_ssm, n_hg, n_gdn = None, None, None
    n_ssm_conv, n_gdn_conv = [], []
    for l in range(DEPTH):
        i = l // 2
        h = _rmsnorm(xt, p["norm_mix"][l], BF16)
        if l % 2 == 0:
            z = _matmul(h, p["w_in_ab"], i, 0, SSM_D_INNER)
            xbc = _matmul(h, p["w_in_ab"], i, AB_OFF_XBC, SSM_CONV_DIM)
            dt = _matmul(h, p["w_in_dt"], i, 0, SSM_HEADS)
            hh = _matmul(h, p["w_in_hg"], i, 0, 4 * HG_WIDTH)
            y, n_ssm, b1 = _ssd(z, xbc, dt, _chunk_transposed(dt, b, n, c), st_ssm_conv[i], st_ssm,
                                n_ssm, p, i, **dims)
            o, n_hg = _gla(hh, st_hg, n_hg, p, i, **dims)
            n_ssm_conv.append(b1)
            mix = jnp.concatenate([y, o], axis=-1)
            xt = _matmul(mix, p["w_out_ab"], i, 0, D_MODEL, epilogue="residual", residual=xt)
        else:
            qkv = _matmul(h, p["w_in_c"], i, 0, GDN_CONV_DIM)
            z = _matmul(h, p["w_in_c"], i, C_OFF_Z, GDN_VAL_DIM)
            ba = _matmul(h, p["w_in_ba"], i, 0, 2 * GDN_V_HEADS)
            o, n_gdn, b3 = _gdn(qkv, z, ba, st_gdn_conv[i], st_gdn, n_gdn, p, i, **dims)
            n_gdn_conv.append(b3)
            xt = _matmul(o, p["w_out_c"], i, 0, D_MODEL, epilogue="residual", residual=xt)
        h = _rmsnorm(xt, p["norm_mlp"][l], BF16)
        u = _matmul(h, p["w_ff1"], l, 0, FFN_HIDDEN, epilogue="relu2", out_dtype=BF16)
        xt = _matmul(u, p["w_ff2"], l, 0, D_MODEL, epilogue="residual", residual=xt)
    y = _rmsnorm(xt, p["norm_final"], F32).reshape(b, lp, d)
    return (y, n_ssm.reshape(st_ssm.shape), jnp.stack(n_ssm_conv), n_hg.reshape(st_hg.shape),
            n_gdn.reshape(st_gdn.shape), jnp.stack(n_gdn_conv))


def kernel(x_prompt, x_sample, state_ssm, state_ssm_conv, state_hgrn, state_gdn, state_gdn_conv, norm_mix, norm_mlp, norm_final, w_in_ab, ssm_conv_w, ssm_conv_b, ssm_dt_bias, ssm_A_log, ssm_D, ssm_norm_w, hg_lower_bounds, hg_norm_w, w_out_ab, w_in_c, gdn_conv_w, gdn_dt_bias, gdn_A_log, gdn_norm_w, w_out_c, w_ff1, w_ff2):
    p = dict(norm_mix=norm_mix, norm_mlp=norm_mlp, norm_final=norm_final, w_in_ab=w_in_ab,
             ssm_conv_w=ssm_conv_w, ssm_conv_b=ssm_conv_b, ssm_dt_bias=ssm_dt_bias, ssm_A_log=ssm_A_log,
             ssm_D=ssm_D, ssm_norm_w=ssm_norm_w, hg_lower_bounds=hg_lower_bounds, hg_norm_w=hg_norm_w,
             w_out_ab=w_out_ab, w_in_c=w_in_c, gdn_conv_w=gdn_conv_w, gdn_dt_bias=gdn_dt_bias,
             gdn_A_log=gdn_A_log, gdn_norm_w=gdn_norm_w, w_out_c=w_out_c, w_ff1=w_ff1, w_ff2=w_ff2)
    p["w_in_dt"] = w_in_ab[:, :, AB_OFF_DT:AB_OFF_HG]
    p["w_in_hg"] = w_in_ab[:, :, AB_OFF_HG:]
    p["w_in_ba"] = w_in_c[:, :, C_OFF_BA:]

    bp, seq, _ = x_prompt.shape
    z_ssm = jnp.zeros((N_AB, bp, SSM_HEADS, SSM_HEAD_DIM, SSM_D_STATE), F32)
    z_ssm_conv = jnp.zeros((N_AB, bp, CONV_W - 1, SSM_CONV_DIM), F32)
    z_hg = jnp.zeros((N_AB, bp, HG_HEADS, HG_HEAD_DIM, HG_HEAD_DIM), F32)
    z_gdn = jnp.zeros((N_C, bp, GDN_V_HEADS, GDN_HEAD_K, GDN_HEAD_V), F32)
    z_gdn_conv = jnp.zeros((N_C, bp, CONV_W - 1, GDN_CONV_DIM), F32)
    c_p = min(CHUNK, seq)
    outs_p = _trunk(x_prompt, z_ssm, z_ssm_conv, z_hg, z_gdn, z_gdn_conv, p, c=c_p, valid=c_p)

    bs, dec, _ = x_sample.shape
    c_s = -(-dec // SUBLANES) * SUBLANES
    xs_pad = jnp.pad(x_sample, ((0, 0), (0, c_s - dec), (0, 0)))
    outs_s = _trunk(xs_pad, state_ssm, state_ssm_conv, state_hgrn, state_gdn, state_gdn_conv, p,
                    c=c_s, valid=dec)
    y_sample = outs_s[0][:, :dec]
    return (outs_p[0], y_sample) + tuple(outs_p[1:]) + tuple(outs_s[1:])
```

```python
import functools
import math

import jax
import jax.numpy as jnp
from jax import lax
from jax.experimental import pallas as pl
from jax.experimental.pallas import tpu as pltpu

F32 = jnp.float32
BF16 = jnp.bfloat16
HIGHEST = lax.Precision.HIGHEST

D_MODEL = 2048
DEPTH = 4
N_AB = (DEPTH + 1) // 2
N_C = DEPTH // 2
CONV_W = 4
CHUNK = 64
EPS = 1e-6
TINY = 1e-30

SSM_D_INNER = D_MODEL
SSM_HEAD_DIM = 64
SSM_HEADS = SSM_D_INNER // SSM_HEAD_DIM
SSM_GROUPS = 4
SSM_D_STATE = 128
SSM_CONV_DIM = SSM_D_INNER + 2 * SSM_GROUPS * SSM_D_STATE

HG_WIDTH = D_MODEL
HG_HEAD_DIM = 128
HG_HEADS = HG_WIDTH // HG_HEAD_DIM

GDN_HEAD_K = 128
GDN_HEAD_V = 128
GDN_K_HEADS = D_MODEL // GDN_HEAD_K
GDN_V_HEADS = 2 * GDN_K_HEADS
GDN_KEY_DIM = GDN_K_HEADS * GDN_HEAD_K
GDN_VAL_DIM = GDN_V_HEADS * GDN_HEAD_V
GDN_CONV_DIM = 2 * GDN_KEY_DIM + GDN_VAL_DIM

FFN_HIDDEN = 4 * D_MODEL

AB_OFF_XBC = SSM_D_INNER
AB_OFF_DT = AB_OFF_XBC + SSM_CONV_DIM
AB_OFF_HG = AB_OFF_DT + SSM_HEADS
C_OFF_Z = GDN_CONV_DIM
C_OFF_BA = C_OFF_Z + GDN_VAL_DIM

SUBLANES = 8
VMEM_LIMIT = 52 * 1024 * 1024
GDN_WAVE = 8
GLA_WAVE = 8


def _params(sem):
    return pltpu.CompilerParams(dimension_semantics=sem, vmem_limit_bytes=VMEM_LIMIT)


def _dot(a, b):
    return jnp.dot(a.astype(BF16), b.astype(BF16), preferred_element_type=F32)


def _dot_nt(a, b):
    return lax.dot_general(a.astype(BF16), b.astype(BF16), (((1,), (1,)), ((), ())),
                           preferred_element_type=F32)


def _dot_tn(a, b):
    return lax.dot_general(a.astype(BF16), b.astype(BF16), (((0,), (0,)), ((), ())),
                           preferred_element_type=F32)


def _split3_bf16(x):
    p0 = x.astype(BF16)
    r1 = x - p0.astype(F32)
    p1 = r1.astype(BF16)
    return p0, p1, (r1 - p1.astype(F32)).astype(BF16)


def _sel_dot(sel, x):
    s = sel.astype(BF16)
    p0, p1, p2 = _split3_bf16(x)
    return (jnp.dot(s, p0, preferred_element_type=F32) + jnp.dot(s, p1, preferred_element_type=F32)
            + jnp.dot(s, p2, preferred_element_type=F32))


def _dot_sel(x, sel):
    s = sel.astype(BF16)
    p0, p1, p2 = _split3_bf16(x)
    return (jnp.dot(p0, s, preferred_element_type=F32) + jnp.dot(p1, s, preferred_element_type=F32)
            + jnp.dot(p2, s, preferred_element_type=F32))


def _sigmoid(x):
    return 1.0 / (1.0 + jnp.exp(-x))


def _silu(x):
    return x * _sigmoid(x)


def _softplus(x):
    return jnp.maximum(x, 0.0) + jnp.log1p(jnp.exp(-jnp.abs(x)))


def _iota2(shape, dim):
    return lax.broadcasted_iota(jnp.int32, shape, dim)


def _causal_conv(xpad_ref, x_ref, buf_ref, w_ref, c):
    @pl.when(pl.program_id(1) == 0)
    def _():
        xpad_ref[5:8, :] = buf_ref[...]

    xpad_ref[8:8 + c, :] = x_ref[...]
    w = w_ref[...]
    y = xpad_ref[5:5 + c, :] * w[0:1, :]
    for j in range(1, CONV_W):
        y = y + xpad_ref[5 + j:5 + j + c, :] * w[j:j + 1, :]
    return y


def _conv_carry(xpad_ref, bufout_ref, c, valid_last):
    @pl.when(pl.program_id(1) == pl.num_programs(1) - 1)
    def _():
        bufout_ref[...] = xpad_ref[8 + valid_last - 3:8 + valid_last, :]

    xpad_ref[5:8, :] = xpad_ref[8 + c - 3:8 + c, :]


def _rmsnorm_kernel(x_ref, w_ref, o_ref):
    x = x_ref[...]
    ms = jnp.mean(x * x, axis=-1, keepdims=True)
    o_ref[...] = ((x * lax.rsqrt(ms + EPS)) * w_ref[...]).astype(o_ref.dtype)


def _rmsnorm(x, w, out_dtype):
    t, d = x.shape
    tm = min(512, t)
    return pl.pallas_call(
        _rmsnorm_kernel,
        grid=(t // tm,),
        in_specs=[pl.BlockSpec((tm, d), lambda i: (i, 0)),
                  pl.BlockSpec((1, d), lambda i: (0, 0))],
        out_specs=pl.BlockSpec((tm, d), lambda i: (i, 0)),
        out_shape=jax.ShapeDtypeStruct((t, d), out_dtype),
        compiler_params=_params(("arbitrary",)),
        name="rmsnorm",
    )(x, w.reshape(1, d))


def _mm_kernel(*refs, epilogue, w_rows_are_outputs):
    if epilogue == "residual":
        a_ref, w_ref, r_ref, o_ref, wb_ref = refs
    else:
        a_ref, w_ref, o_ref, wb_ref = refs

    @pl.when(pl.program_id(1) == 0)
    def _():
        wb_ref[...] = w_ref[...].astype(BF16)

    if w_rows_are_outputs:
        acc = lax.dot_general(a_ref[...], wb_ref[...], (((1,), (1,)), ((), ())),
                              preferred_element_type=F32)
    else:
        acc = jnp.dot(a_ref[...], wb_ref[...], preferred_element_type=F32)
    if epilogue == "relu2":
        acc = jnp.square(jnp.maximum(acc, 0.0))
    elif epilogue == "residual":
        acc = r_ref[...] + acc
    o_ref[...] = acc.astype(o_ref.dtype)


def _mm_tiles(t, k, n):
    if n % 128 != 0:
        return min(1024, t), n, 2
    tn = {2048: 1024, 4096: 512, 8192: 512}[k]
    tm = {2048: 1024, 4096: 512, 8192: 512}[k]
    return min(tm, t), min(tn, n), (1 if k == 8192 else 2)


def _matmul(a, w3, layer, col0, n, *, epilogue="none", residual=None, out_dtype=F32,
            w_rows_are_outputs=False):
    t, k = a.shape
    tm, tn, w_bufs = _mm_tiles(t, k, n)
    if w_rows_are_outputs:
        n_total = w3.shape[1]
        row0 = layer * n_total + col0
        w3 = w3.reshape(w3.shape[0] * n_total, k)
        w_spec = pl.BlockSpec((pl.Element(tn), pl.Element(k)), lambda j, i: (pl.multiple_of(row0 + j * tn, SUBLANES), 0),
                              pipeline_mode=pl.Buffered(w_bufs))
        wb_shape = (tn, k)
    else:
        cb0 = col0 // tn
        w_spec = pl.BlockSpec((None, k, tn), lambda j, i: (layer, 0, cb0 + j),
                              pipeline_mode=pl.Buffered(w_bufs))
        wb_shape = (k, tn)
    in_specs = [pl.BlockSpec((tm, k), lambda j, i: (i, 0)), w_spec]
    args = [a, w3]
    if epilogue == "residual":
        in_specs.append(pl.BlockSpec((tm, tn), lambda j, i: (i, j)))
        args.append(residual)
    return pl.pallas_call(
        functools.partial(_mm_kernel, epilogue=epilogue, w_rows_are_outputs=w_rows_are_outputs),
        grid=(n // tn, t // tm),
        in_specs=in_specs,
        out_specs=pl.BlockSpec((tm, tn), lambda j, i: (i, j)),
        out_shape=jax.ShapeDtypeStruct((t, n), out_dtype),
        scratch_shapes=[pltpu.VMEM(wb_shape, BF16)],
        compiler_params=_params(("arbitrary", "arbitrary")),
        name="matmul_" + epilogue,
    )(*args)


def _ssd_kernel(z_ref, xbc_ref, dt_ref, dtT_ref, buf_ref, s0_ref, convw_ref, convb_ref,
                dtb_ref, dtbT_ref, alog_ref, alogT_ref, dskip_ref, normw_ref,
                y_ref, s_ref, bufout_ref, xpad_ref, *, c, valid):
    H, P, N, G = SSM_HEADS, SSM_HEAD_DIM, SSM_D_STATE, SSM_GROUPS
    R = H // G

    @pl.when(pl.program_id(1) == 0)
    def _():
        s_ref[...] = s0_ref[...]

    xbc = _silu(_causal_conv(xpad_ref, xbc_ref, buf_ref, convw_ref, c) + convb_ref[...])
    _conv_carry(xpad_ref, bufout_ref, c, valid)

    row = _iota2((c, c), 0)
    col = _iota2((c, c), 1)
    tri = col <= row
    ltri = tri.astype(F32)
    utri = (row <= col).astype(F32)

    dt = _softplus(dt_ref[...] + dtb_ref[...])
    dtT = _softplus(dtT_ref[...] + dtbT_ref[...])
    la = dt * (-jnp.exp(alog_ref[...]))
    laT = dtT * (-jnp.exp(alogT_ref[...]))
    xs = xbc[:, :SSM_D_INNER]
    bm = xbc[:, SSM_D_INNER:SSM_D_INNER + G * N]
    cm = xbc[:, SSM_D_INNER + G * N:]
    if valid < c:
        rmask = _iota2((c, 1), 0) < valid
        cmask = _iota2((1, c), 1) < valid
        la = jnp.where(rmask, la, 0.0)
        laT = jnp.where(cmask, laT, 0.0)
        dt = jnp.where(rmask, dt, 0.0)
        bm = jnp.where(rmask, bm, 0.0)
        cm = jnp.where(rmask, cm, 0.0)
    acum = _sel_dot(ltri, la)
    acumT = _dot_sel(laT, utri)
    alast = acum[c - 1:c, :]

    expand = (jnp.right_shift(_iota2((H, SSM_D_INNER), 1), int(math.log2(P)))
              == _iota2((H, SSM_D_INNER), 0)).astype(F32)
    xdt = xs * _dot_sel(dt, expand)
    ea_x = jnp.exp(_dot_sel(acum, expand))
    xw = xdt * jnp.exp(_dot_sel(alast - acum, expand))
    dskip = _dot_sel(jnp.broadcast_to(dskip_ref[...], (SUBLANES, H)), expand)[0:1, :]

    ys = []
    for g in range(G):
        bg = bm[:, g * N:(g + 1) * N]
        cg = cm[:, g * N:(g + 1) * N]
        cb = _dot_nt(cg, bg)
        sg = s_ref[g * R * P:(g + 1) * R * P, :]
        y_inter = _dot_nt(cg, sg)
        upd = _dot_tn(xw[:, g * R * P:(g + 1) * R * P], bg)
        y_intra = []
        for r in range(R):
            h = g * R + r
            diff = acum[:, h:h + 1] - acumT[h:h + 1, :]
            decay = jnp.where(tri, jnp.exp(jnp.where(tri, diff, 0.0)), 0.0)
            y_intra.append(_dot(cb * decay, xdt[:, h * P:(h + 1) * P]))
            sl = slice(h * P, (h + 1) * P)
            s_ref[sl, :] = jnp.exp(alast[:, h:h + 1]) * s_ref[sl, :] + upd[r * P:(r + 1) * P, :]
        ys.append(jnp.concatenate(y_intra, axis=1) + y_inter * ea_x[:, g * R * P:(g + 1) * R * P])
    y = jnp.concatenate(ys, axis=1) + dskip * xs
    y = y * _silu(z_ref[...])
    gw = SSM_D_INNER // G
    outs = []
    for g in range(G):
        yg = y[:, g * gw:(g + 1) * gw]
        outs.append(yg * lax.rsqrt(jnp.mean(yg * yg, axis=-1, keepdims=True) + EPS))
    y_ref[...] = (jnp.concatenate(outs, axis=1) * normw_ref[...]).astype(y_ref.dtype)


def _layer_state_io(s_all, acc, layer, rows, cols):
    n_layers, b = s_all.shape[:2]
    spec = pl.BlockSpec((None, None, rows, cols), lambda bb, ii: (layer, bb, 0, 0))
    shape = jax.ShapeDtypeStruct((n_layers, b, rows, cols), F32)
    extra_args, extra_specs = ([], []) if acc is None else ([acc], [pl.BlockSpec(memory_space=pl.ANY)])
    return s_all.reshape(n_layers, b, rows, cols), spec, shape, extra_args, extra_specs


def _without_ref(kern, pos, present):
    if not present:
        return kern
    return lambda *refs: kern(*refs[:pos], *refs[pos + 1:])


def _ssd(z, xbc, dt, dtT, buf, s_all, s_acc, p, i, *, b, n, c, valid):
    t = b * n * c
    H, P, N = SSM_HEADS, SSM_HEAD_DIM, SSM_D_STATE
    C = SSM_CONV_DIM
    tok = lambda w: pl.BlockSpec((c, w), lambda bb, ii: (bb * n + ii, 0))
    full2 = lambda a: pl.BlockSpec(a.shape, lambda bb, ii: (0, 0))
    convw = p["ssm_conv_w"][i]
    convb = p["ssm_conv_b"][i].reshape(1, C)
    dtb = p["ssm_dt_bias"][i].reshape(1, H)
    dtbT = p["ssm_dt_bias"][i].reshape(H, 1)
    alog = p["ssm_A_log"][i].reshape(1, H)
    alogT = p["ssm_A_log"][i].reshape(H, 1)
    dskip = p["ssm_D"][i].reshape(1, H)
    normw = p["ssm_norm_w"][i].reshape(1, SSM_D_INNER)
    small = [convw, convb, dtb, dtbT, alog, alogT, dskip, normw]
    s_in, s_spec, s_shape, acc_args, acc_specs = _layer_state_io(s_all, s_acc, i, H * P, N)
    n_in = 6 + len(small)
    y, s_new, buf_new = pl.pallas_call(
        _without_ref(functools.partial(_ssd_kernel, c=c, valid=valid), n_in, bool(acc_args)),
        grid=(b, n),
        in_specs=[tok(SSM_D_INNER), tok(C), tok(H),
                  pl.BlockSpec((None, None, H, c), lambda bb, ii: (bb, ii, 0, 0)),
                  pl.BlockSpec((None, CONV_W - 1, C), lambda bb, ii: (bb, 0, 0)),
                  s_spec]
                 + [full2(a) for a in small] + acc_specs,
        out_specs=[tok(SSM_D_INNER), s_spec,
                   pl.BlockSpec((None, CONV_W - 1, C), lambda bb, ii: (bb, 0, 0))],
        out_shape=[jax.ShapeDtypeStruct((t, SSM_D_INNER), BF16), s_shape,
                   jax.ShapeDtypeStruct((b, CONV_W - 1, C), F32)],
        scratch_shapes=[pltpu.VMEM((8 + c, C), F32)],
        input_output_aliases={n_in: 1} if acc_args else {},
        compiler_params=_params(("arbitrary", "arbitrary")),
        name="ssd_scan",
    )(z, xbc, dt, dtT, buf, s_in, *small, *acc_args)
    return y, s_new, buf_new


def _gla_kernel(hq_ref, hf_ref, hi_ref, hg_ref, s0_ref, lb_ref, normw_ref,
                o_ref, s_ref, *, c, valid, layer):
    H, K = HG_HEADS, HG_HEAD_DIM

    @pl.when(pl.program_id(1) == 0)
    def _():
        s_ref[...] = s0_ref[...]

    lbp = lb_ref[...]
    mx = lbp[0:1, :]
    for j in range(1, N_AB):
        mx = jnp.maximum(mx, lbp[j:j + 1, :])
    e = [jnp.exp(lbp[j:j + 1, :] - mx) for j in range(N_AB)]
    den = e[0]
    for j in range(1, N_AB):
        den = den + e[j]
    lb = e[0] / den
    for j in range(1, layer + 1):
        lb = lb + e[j] / den
    lb = lb - e[0] / den

    q = _silu(hq_ref[...])
    sig = _sigmoid(hf_ref[...])
    f = lb + (1.0 - lb) * sig
    logf = jnp.log(jnp.maximum(f, TINY))
    k = (1.0 - lb) * (1.0 - sig)
    v = hi_ref[...]
    if valid < c:
        rmask = _iota2((c, 1), 0) < valid
        q = jnp.where(rmask, q, 0.0)
        k = jnp.where(rmask, k, 0.0)
        v = jnp.where(rmask, v, 0.0)
        logf = jnp.where(rmask, logf, 0.0)

    row = _iota2((c, c), 0)
    col = _iota2((c, c), 1)
    ltri = (col <= row).astype(F32)
    g = _sel_dot(ltri, logf)
    glast = g[c - 1:c, :]

    rowv = _iota2((c, 1), 0)
    kb16 = k.astype(BF16)
    lhs, rhs, masks = [q.astype(BF16)], [kb16], [row == col]
    m = c // 2
    while m >= 1:
        sh2m = int(math.log2(2 * m))
        blk_r = jnp.right_shift(row, sh2m)
        sel = (col == jnp.left_shift(blk_r, sh2m) + m).astype(F32)
        d = g - _sel_dot(sel, g)
        upper = jnp.bitwise_and(rowv, 2 * m - 1) >= m
        zl = (jnp.where(upper, q, k) * jnp.exp(jnp.where(upper, d, -d))).astype(BF16)
        lhs.append(zl)
        rhs.append(zl)
        t_up = jnp.bitwise_and(row, 2 * m - 1) - m
        s_lo = (m - 1) - jnp.bitwise_and(col, 2 * m - 1)
        other_blk = -jnp.bitwise_xor(blk_r, jnp.right_shift(col, sh2m))
        masks.append(jnp.bitwise_or(jnp.bitwise_or(t_up, s_lo), other_blk) >= 0)
        m //= 2

    qg = (q * jnp.exp(g)).astype(BF16)
    kdec = (k * jnp.exp(glast - g)).astype(BF16)
    vb16 = v.astype(BF16)
    glT = jnp.broadcast_to(glast, (SUBLANES, H * K))
    outs = [None] * H
    for h0 in range(0, H, GLA_WAVE):
        heads = range(h0, min(h0 + GLA_WAVE, H))
        att = {}
        for h in heads:
            sl = slice(h * K, (h + 1) * K)
            a = jnp.zeros((c, c), F32)
            for zl, zr, ml in zip(lhs, rhs, masks):
                a = a + jnp.where(ml, _dot_nt(zl[:, sl], zr[:, sl]), 0.0)
            att[h] = a
        for h in heads:
            sl = slice(h * K, (h + 1) * K)
            sh = s_ref[sl, :]
            y = _dot(att[h], vb16[:, sl]) + _dot(qg[:, sl], sh)
            dec_col = jnp.exp(glT[:, sl].T[:, 0:1])
            s_ref[sl, :] = dec_col * sh + _dot_tn(kdec[:, sl], vb16[:, sl])
            outs[h] = y * lax.rsqrt(jnp.mean(y * y, axis=-1, keepdims=True) + EPS)
    o = jnp.concatenate(outs, axis=1) * normw_ref[...]
    o_ref[...] = (o * _silu(hg_ref[...])).astype(o_ref.dtype)


def _gla(hh, s_all, s_acc, p, i, *, b, n, c, valid):
    t = b * n * c
    H, K = HG_HEADS, HG_HEAD_DIM
    W = HG_WIDTH
    tok = lambda j: pl.BlockSpec((c, W), lambda bb, ii: (bb * n + ii, j))
    full2 = lambda a: pl.BlockSpec(a.shape, lambda bb, ii: (0, 0))
    lb = p["hg_lower_bounds"]
    normw = p["hg_norm_w"][i].reshape(1, W)
    s_in, s_spec, s_shape, acc_args, acc_specs = _layer_state_io(s_all, s_acc, i, H * K, K)
    n_in = 7
    o, s_new = pl.pallas_call(
        _without_ref(functools.partial(_gla_kernel, c=c, valid=valid, layer=i), n_in, bool(acc_args)),
        grid=(b, n),
        in_specs=[tok(0), tok(1), tok(2), tok(3), s_spec, full2(lb), full2(normw)] + acc_specs,
        out_specs=[pl.BlockSpec((c, W), lambda bb, ii: (bb * n + ii, 0)), s_spec],
        out_shape=[jax.ShapeDtypeStruct((t, W), BF16), s_shape],
        input_output_aliases={n_in: 1} if acc_args else {},
        compiler_params=_params(("arbitrary", "arbitrary")),
        name="gla_scan",
    )(hh, hh, hh, hh, s_in, lb, normw, *acc_args)
    return o, s_new


def _split_bf16(x):
    hi = x.astype(BF16)
    return hi, (x - hi.astype(F32)).astype(BF16)


def _gdn_kernel(qkv_ref, z_ref, ba_ref, aT2_ref, buf_ref, s0_ref, convw_ref,
                dtb_ref, dtb2_ref, alog_ref, alog2_ref, normw_ref,
                o_ref, s_ref, bufout_ref, xpad_ref, *, c, valid, wave):
    HV, HK, K, V = GDN_V_HEADS, GDN_K_HEADS, GDN_HEAD_K, GDN_HEAD_V
    L = 2 * c
    log2c = int(math.log2(c))

    @pl.when(pl.program_id(1) == 0)
    def _():
        s_ref[...] = s0_ref[...]

    qkv = _silu(_causal_conv(xpad_ref, qkv_ref, buf_ref, convw_ref, c))
    _conv_carry(xpad_ref, bufout_ref, c, valid)

    row = _iota2((c, c), 0)
    col = _iota2((c, c), 1)
    ltri = (col <= row).astype(F32)
    rowp = _iota2((c, L), 0)
    lanep = _iota2((c, L), 1)
    left = lanep < c
    sp = jnp.bitwise_and(lanep, c - 1)
    tri2 = sp <= rowp
    strict2 = sp < rowp
    eye2 = (sp == rowp).astype(F32)
    rl = _iota2((L, L), 0)
    cl = _iota2((L, L), 1)
    utri2 = jnp.where(jnp.right_shift(rl, log2c) == jnp.right_shift(cl, log2c),
                      (rl <= cl).astype(F32), 0.0)

    ba = ba_ref[...]
    beta = _sigmoid(ba[:, :HV])
    gl = -jnp.exp(alog_ref[...]) * _softplus(ba[:, HV:] + dtb_ref[...])
    glT2 = -jnp.exp(alog2_ref[...]) * _softplus(aT2_ref[...] + dtb2_ref[...])
    if valid < c:
        rmask = _iota2((c, 1), 0) < valid
        cmask = jnp.bitwise_and(_iota2((1, L), 1), c - 1) < valid
        beta = jnp.where(rmask, beta, 0.0)
        gl = jnp.where(rmask, gl, 0.0)
        glT2 = jnp.where(cmask, glT2, 0.0)
        qkv = jnp.where(rmask, qkv, 0.0)
    gcum = _sel_dot(ltri, gl)
    gcumT2 = _dot_sel(glT2, utri2)
    egc = jnp.exp(gcum)
    glast = gcum[c - 1:c, :]
    elast = jnp.exp(glast)
    edec = jnp.exp(glast - gcum)

    n_sq = log2c - 1
    normw = normw_ref[...]
    zeros_cv = jnp.zeros((c, V), F32)

    def pair_cols(a, hk):
        return jnp.where(left, a[:, 2 * hk:2 * hk + 1], a[:, 2 * hk + 1:2 * hk + 2])

    def block_diag(x2):
        return jnp.concatenate([jnp.where(left, x2, 0.0), jnp.where(left, 0.0, x2)], axis=0)

    def pair_product(lhs_split, x2):
        lh, ll = lhs_split
        rh, rlo = _split_bf16(block_diag(x2))
        return (jnp.dot(lh, rh, preferred_element_type=F32)
                + jnp.dot(ll, rh, preferred_element_type=F32)
                + jnp.dot(lh, rlo, preferred_element_type=F32))

    outs = [None] * HV
    for w0 in range(0, HK, wave):
        hks = list(range(w0, min(w0 + wave, HK)))
        qn, kn, gam, qkg, pw, tm = {}, {}, {}, {}, {}, {}
        for hk in hks:
            qh = qkv[:, hk * K:(hk + 1) * K]
            kh = qkv[:, GDN_KEY_DIM + hk * K:GDN_KEY_DIM + (hk + 1) * K]
            qn[hk] = qh * lax.rsqrt(jnp.sum(qh * qh, axis=-1, keepdims=True) + EPS) * (K ** -0.5)
            kn[hk] = kh * lax.rsqrt(jnp.sum(kh * kh, axis=-1, keepdims=True) + EPS)
            gq = _dot_nt(jnp.concatenate([kn[hk], qn[hk]], axis=0),
                         jnp.concatenate([kn[hk], kn[hk]], axis=0))
            diff = pair_cols(gcum, hk) - gcumT2[hk:hk + 1, :]
            gam[hk] = jnp.where(tri2, jnp.exp(jnp.where(tri2, diff, 0.0)), 0.0)
            pw[hk] = -jnp.where(strict2, pair_cols(beta, hk) * gq[:c] * gam[hk], 0.0)
            qkg[hk] = gq[c:] * gam[hk]
            tm[hk] = eye2 + pw[hk]
        for _ in range(n_sq):
            for hk in hks:
                pw[hk] = pair_product(_split_bf16(pw[hk]), pw[hk])
            for hk in hks:
                tm[hk] = tm[hk] + pair_product(_split_bf16(tm[hk]), pw[hk])
        wu, ws_qs, v_new = {}, {}, {}
        for hk in hks:
            rhs = []
            for j in range(2):
                h = 2 * hk + j
                bh = beta[:, h:h + 1]
                vh = qkv[:, 2 * GDN_KEY_DIM + h * V:2 * GDN_KEY_DIM + (h + 1) * V]
                rhs.append(jnp.concatenate([kn[hk] * (bh * egc[:, h:h + 1]), vh * bh], axis=1))
            wu[hk] = _dot(block_diag(tm[hk]), jnp.concatenate(rhs, axis=0))
        for hk in hks:
            for j in range(2):
                h = 2 * hk + j
                wq = jnp.concatenate([wu[hk][j * c:(j + 1) * c, :K], qn[hk] * egc[:, h:h + 1]], axis=0)
                ws_qs[h] = _dot(wq, s_ref[h * K:(h + 1) * K, :])
                v_new[h] = wu[hk][j * c:(j + 1) * c, K:] - ws_qs[h][:c]
        for hk in hks:
            ha, hb = 2 * hk, 2 * hk + 1
            y2 = _dot(block_diag(qkg[hk]), jnp.concatenate([v_new[ha], v_new[hb]], axis=0))
            kdec = jnp.concatenate([kn[hk] * edec[:, ha:ha + 1], kn[hk] * edec[:, hb:hb + 1]], axis=0)
            vbd = jnp.concatenate([jnp.concatenate([v_new[ha], zeros_cv], axis=1),
                                   jnp.concatenate([zeros_cv, v_new[hb]], axis=1)], axis=0)
            upd = _dot_tn(kdec, vbd)
            for j, h in enumerate((ha, hb)):
                sl = slice(h * K, (h + 1) * K)
                s_ref[sl, :] = elast[:, h:h + 1] * s_ref[sl, :] + upd[:, j * V:(j + 1) * V]
                y = ws_qs[h][c:] + y2[j * c:(j + 1) * c]
                yn = (y * lax.rsqrt(jnp.mean(y * y, axis=-1, keepdims=True) + EPS)) * normw
                outs[h] = yn * _silu(z_ref[:, h * V:(h + 1) * V])
    o_ref[...] = jnp.concatenate(outs, axis=1).astype(o_ref.dtype)


def _pair_packed_transpose(a, b, n, c):
    hh = a.shape[-1] // 2
    x = a.reshape(b, n, c, hh, 2)
    return jnp.transpose(x, (0, 1, 3, 4, 2)).reshape(b, n, hh, 2 * c)


def _gdn(qkv, z, ba, buf, s_all, s_acc, p, i, *, b, n, c, valid):
    t = b * n * c
    HV, HK, K, V = GDN_V_HEADS, GDN_K_HEADS, GDN_HEAD_K, GDN_HEAD_V
    C = GDN_CONV_DIM
    tok = lambda w: pl.BlockSpec((c, w), lambda bb, ii: (bb * n + ii, 0))
    full2 = lambda a: pl.BlockSpec(a.shape, lambda bb, ii: (0, 0))
    aT2 = _pair_packed_transpose(ba[:, HV:], b, n, c)
    convw = p["gdn_conv_w"][i]
    dtb = p["gdn_dt_bias"][i].reshape(1, HV)
    dtb2 = jnp.repeat(p["gdn_dt_bias"][i].reshape(HK, 2), c, axis=1)
    alog = p["gdn_A_log"][i].reshape(1, HV)
    alog2 = jnp.repeat(p["gdn_A_log"][i].reshape(HK, 2), c, axis=1)
    normw = p["gdn_norm_w"][i].reshape(1, V)
    small = [convw, dtb, dtb2, alog, alog2, normw]
    s_in, s_spec, s_shape, acc_args, acc_specs = _layer_state_io(s_all, s_acc, i, HV * K, V)
    n_in = 6 + len(small)
    o, s_new, buf_new = pl.pallas_call(
        _without_ref(functools.partial(_gdn_kernel, c=c, valid=valid, wave=GDN_WAVE), n_in, bool(acc_args)),
        grid=(b, n),
        in_specs=[tok(C), tok(GDN_VAL_DIM), tok(2 * HV),
                  pl.BlockSpec((None, None, HK, 2 * c), lambda bb, ii: (bb, ii, 0, 0)),
                  pl.BlockSpec((None, CONV_W - 1, C), lambda bb, ii: (bb, 0, 0)),
                  s_spec]
                 + [full2(a) for a in small] + acc_specs,
        out_specs=[tok(GDN_VAL_DIM), s_spec,
                   pl.BlockSpec((None, CONV_W - 1, C), lambda bb, ii: (bb, 0, 0))],
        out_shape=[jax.ShapeDtypeStruct((t, GDN_VAL_DIM), BF16), s_shape,
                   jax.ShapeDtypeStruct((b, CONV_W - 1, C), F32)],
        scratch_shapes=[pltpu.VMEM((8 + c, C), F32)],
        input_output_aliases={n_in: 1} if acc_args else {},
        compiler_params=_params(("arbitrary", "arbitrary")),
        name="gdn_scan",
    )(qkv, z, ba, aT2, buf, s_in, *small, *acc_args)
    return o, s_new, buf_new


def _chunk_transposed(a, b, n, c):
    return jnp.swapaxes(a.reshape(b, n, c, a.shape[-1]), 2, 3)


def _trunk(x, st_ssm, st_ssm_conv, st_hg, st_gdn, st_gdn_conv, p, *, c, valid):
    b, lp, d = x.shape
    n = lp // c
    t = b * lp
    xt = x.reshape(t, d)
    dims = dict(b=b, n=n, c=c, valid=valid)
    n_ssm, n_hg, n_gdn = None, None, None
    n_ssm_conv, n_gdn_conv = [], []
    for l in range(DEPTH):
        i = l // 2
        h = _rmsnorm(xt, p["norm_mix"][l], BF16)
        if l % 2 == 0:
            in_proj = functools.partial(_matmul, h, p["w_in_ab_t"], i, w_rows_are_outputs=True)
            z = in_proj(0, SSM_D_INNER)
            xbc = in_proj(AB_OFF_XBC, SSM_CONV_DIM)
            dt = in_proj(AB_OFF_DT, SSM_HEADS)
            hh = in_proj(AB_OFF_HG, 4 * HG_WIDTH)
            y, n_ssm, b1 = _ssd(z, xbc, dt, _chunk_transposed(dt, b, n, c), st_ssm_conv[i], st_ssm,
                                n_ssm, p, i, **dims)
            o, n_hg = _gla(hh, st_hg, n_hg, p, i, **dims)
            n_ssm_conv.append(b1)
            mix = jnp.concatenate([y, o], axis=-1)
            xt = _matmul(mix, p["w_out_ab"], i, 0, D_MODEL, epilogue="residual", residual=xt)
        else:
            in_proj = functools.partial(_matmul, h, p["w_in_c_t"], i, w_rows_are_outputs=True)
            qkv = in_proj(0, GDN_CONV_DIM)
            z = in_proj(C_OFF_Z, GDN_VAL_DIM)
            ba = in_proj(C_OFF_BA, 2 * GDN_V_HEADS)
            o, n_gdn, b3 = _gdn(qkv, z, ba, st_gdn_conv[i], st_gdn, n_gdn, p, i, **dims)
            n_gdn_conv.append(b3)
            xt = _matmul(o, p["w_out_c"], i, 0, D_MODEL, epilogue="residual", residual=xt)
        h = _rmsnorm(xt, p["norm_mlp"][l], BF16)
        u = _matmul(h, p["w_ff1"], l, 0, FFN_HIDDEN, epilogue="relu2", out_dtype=BF16)
        xt = _matmul(u, p["w_ff2"], l, 0, D_MODEL, epilogue="residual", residual=xt)
    y = _rmsnorm(xt, p["norm_final"], F32).reshape(b, lp, d)
    return (y, n_ssm.reshape(st_ssm.shape), jnp.stack(n_ssm_conv), n_hg.reshape(st_hg.shape),
            n_gdn.reshape(st_gdn.shape), jnp.stack(n_gdn_conv))


def kernel(x_prompt, x_sample, state_ssm, state_ssm_conv, state_hgrn, state_gdn, state_gdn_conv, norm_mix, norm_mlp, norm_final, w_in_ab, ssm_conv_w, ssm_conv_b, ssm_dt_bias, ssm_A_log, ssm_D, ssm_norm_w, hg_lower_bounds, hg_norm_w, w_out_ab, w_in_c, gdn_conv_w, gdn_dt_bias, gdn_A_log, gdn_norm_w, w_out_c, w_ff1, w_ff2):
    p = dict(norm_mix=norm_mix, norm_mlp=norm_mlp, norm_final=norm_final, w_in_ab=w_in_ab,
             ssm_conv_w=ssm_conv_w, ssm_conv_b=ssm_conv_b, ssm_dt_bias=ssm_dt_bias, ssm_A_log=ssm_A_log,
             ssm_D=ssm_D, ssm_norm_w=ssm_norm_w, hg_lower_bounds=hg_lower_bounds, hg_norm_w=hg_norm_w,
             w_out_ab=w_out_ab, w_in_c=w_in_c, gdn_conv_w=gdn_conv_w, gdn_dt_bias=gdn_dt_bias,
             gdn_A_log=gdn_A_log, gdn_norm_w=gdn_norm_w, w_out_c=w_out_c, w_ff1=w_ff1, w_ff2=w_ff2)
    p["w_in_ab_t"] = jnp.swapaxes(w_in_ab, 1, 2)
    p["w_in_c_t"] = jnp.swapaxes(w_in_c, 1, 2)

    bp, seq, _ = x_prompt.shape
    z_ssm = jnp.zeros((N_AB, bp, SSM_HEADS, SSM_HEAD_DIM, SSM_D_STATE), F32)
    z_ssm_conv = jnp.zeros((N_AB, bp, CONV_W - 1, SSM_CONV_DIM), F32)
    z_hg = jnp.zeros((N_AB, bp, HG_HEADS, HG_HEAD_DIM, HG_HEAD_DIM), F32)
    z_gdn = jnp.zeros((N_C, bp, GDN_V_HEADS, GDN_HEAD_K, GDN_HEAD_V), F32)
    z_gdn_conv = jnp.zeros((N_C, bp, CONV_W - 1, GDN_CONV_DIM), F32)
    c_p = min(CHUNK, seq)
    outs_p = _trunk(x_prompt, z_ssm, z_ssm_conv, z_hg, z_gdn, z_gdn_conv, p, c=c_p, valid=c_p)

    bs, dec, _ = x_sample.shape
    c_s = -(-dec // SUBLANES) * SUBLANES
    xs_pad = jnp.pad(x_sample, ((0, 0), (0, c_s - dec), (0, 0)))
    outs_s = _trunk(xs_pad, state_ssm, state_ssm_conv, state_hgrn, state_gdn, state_gdn_conv, p,
                    c=c_s, valid=dec)
    y_sample = outs_s[0][:, :dec]
    return (outs_p[0], y_sample) + tuple(outs_p[1:]) + tuple(outs_s[1:])
```

```python
import functools
import math

import jax
import jax.numpy as jnp
from jax import lax
from jax.experimental import pallas as pl
from jax.experimental.pallas import tpu as pltpu

F32 = jnp.float32
BF16 = jnp.bfloat16
HIGHEST = lax.Precision.HIGHEST

D_MODEL = 2048
DEPTH = 4
N_AB = (DEPTH + 1) // 2
N_C = DEPTH // 2
CONV_W = 4
CHUNK = 64
EPS = 1e-6
TINY = 1e-30

SSM_D_INNER = D_MODEL
SSM_HEAD_DIM = 64
SSM_HEADS = SSM_D_INNER // SSM_HEAD_DIM
SSM_GROUPS = 4
SSM_D_STATE = 128
SSM_CONV_DIM = SSM_D_INNER + 2 * SSM_GROUPS * SSM_D_STATE

HG_WIDTH = D_MODEL
HG_HEAD_DIM = 128
HG_HEADS = HG_WIDTH // HG_HEAD_DIM

GDN_HEAD_K = 128
GDN_HEAD_V = 128
GDN_K_HEADS = D_MODEL // GDN_HEAD_K
GDN_V_HEADS = 2 * GDN_K_HEADS
GDN_KEY_DIM = GDN_K_HEADS * GDN_HEAD_K
GDN_VAL_DIM = GDN_V_HEADS * GDN_HEAD_V
GDN_CONV_DIM = 2 * GDN_KEY_DIM + GDN_VAL_DIM

FFN_HIDDEN = 4 * D_MODEL

AB_OFF_XBC = SSM_D_INNER
AB_OFF_DT = AB_OFF_XBC + SSM_CONV_DIM
AB_OFF_HG = AB_OFF_DT + SSM_HEADS
C_OFF_Z = GDN_CONV_DIM
C_OFF_BA = C_OFF_Z + GDN_VAL_DIM

SUBLANES = 8
VMEM_LIMIT = 52 * 1024 * 1024
GDN_WAVE = 8
GLA_WAVE = 8
SSD_WAVE = 2


def _params(sem):
    return pltpu.CompilerParams(dimension_semantics=sem, vmem_limit_bytes=VMEM_LIMIT)


def _dot(a, b):
    return jnp.dot(a.astype(BF16), b.astype(BF16), preferred_element_type=F32)


def _dot_nt(a, b):
    return lax.dot_general(a.astype(BF16), b.astype(BF16), (((1,), (1,)), ((), ())),
                           preferred_element_type=F32)


def _dot_tn(a, b):
    return lax.dot_general(a.astype(BF16), b.astype(BF16), (((0,), (0,)), ((), ())),
                           preferred_element_type=F32)


def _split3_bf16(x):
    p0 = x.astype(BF16)
    r1 = x - p0.astype(F32)
    p1 = r1.astype(BF16)
    return p0, p1, (r1 - p1.astype(F32)).astype(BF16)


def _sel_dot(sel, x):
    s = sel.astype(BF16)
    p0, p1, p2 = _split3_bf16(x)
    return (jnp.dot(s, p0, preferred_element_type=F32) + jnp.dot(s, p1, preferred_element_type=F32)
            + jnp.dot(s, p2, preferred_element_type=F32))


def _dot_sel(x, sel):
    s = sel.astype(BF16)
    p0, p1, p2 = _split3_bf16(x)
    return (jnp.dot(p0, s, preferred_element_type=F32) + jnp.dot(p1, s, preferred_element_type=F32)
            + jnp.dot(p2, s, preferred_element_type=F32))


def _sigmoid(x):
    return 1.0 / (1.0 + jnp.exp(-x))


def _silu(x):
    return x * _sigmoid(x)


def _softplus(x):
    return jnp.maximum(x, 0.0) + jnp.log1p(jnp.exp(-jnp.abs(x)))


def _iota2(shape, dim):
    return lax.broadcasted_iota(jnp.int32, shape, dim)


def _block_mid_rows(g, m, rowv):
    c, w = g.shape
    if 2 * m >= SUBLANES:
        return jnp.concatenate([jnp.broadcast_to(g[b0 + m:b0 + m + 1, :], (2 * m, w))
                                for b0 in range(0, c, 2 * m)], axis=0)
    pos = jnp.bitwise_and(rowv, 2 * m - 1)
    out = g
    for off in range(1 - m, m + 1):
        if off != 0:
            out = jnp.where(pos == m - off, pltpu.roll(g, (-off) % c, 0), out)
    return out


def _conv_begin(xpad_ref, x_ref, buf_ref, c):
    @pl.when(pl.program_id(1) == 0)
    def _():
        xpad_ref[5:8, :] = buf_ref[...]

    xpad_ref[8:8 + c, :] = x_ref[...]


def _conv_tile(xpad_ref, w_ref, c, col0, width):
    sl = slice(col0, col0 + width)
    y = xpad_ref[5:5 + c, sl] * w_ref[0:1, sl]
    for j in range(1, CONV_W):
        y = y + xpad_ref[5 + j:5 + j + c, sl] * w_ref[j:j + 1, sl]
    return y


def _conv_carry(xpad_ref, bufout_ref, c, valid_last):
    @pl.when(pl.program_id(1) == pl.num_programs(1) - 1)
    def _():
        bufout_ref[...] = xpad_ref[8 + valid_last - 3:8 + valid_last, :]

    xpad_ref[5:8, :] = xpad_ref[8 + c - 3:8 + c, :]


def _rmsnorm_kernel(x_ref, w_ref, o_ref):
    x = x_ref[...]
    ms = jnp.mean(x * x, axis=-1, keepdims=True)
    o_ref[...] = ((x * lax.rsqrt(ms + EPS)) * w_ref[...]).astype(o_ref.dtype)


def _rmsnorm(x, w, out_dtype, row0=0, rows=None):
    d = x.shape[1]
    t = x.shape[0] if rows is None else rows
    tm = min(512, t)
    blk0 = row0 // tm
    return pl.pallas_call(
        _rmsnorm_kernel,
        grid=(t // tm,),
        in_specs=[pl.BlockSpec((tm, d), lambda i: (blk0 + i, 0)),
                  pl.BlockSpec((1, d), lambda i: (0, 0))],
        out_specs=pl.BlockSpec((tm, d), lambda i: (i, 0)),
        out_shape=jax.ShapeDtypeStruct((t, d), out_dtype),
        compiler_params=_params(("arbitrary",)),
        name="rmsnorm",
    )(x, w.reshape(1, d))


def _mm_kernel(*refs, epilogue, w_rows_are_outputs):
    if epilogue == "residual":
        a_ref, w_ref, r_ref, o_ref, wb_ref = refs
    else:
        a_ref, w_ref, o_ref, wb_ref = refs

    @pl.when(pl.program_id(1) == 0)
    def _():
        wb_ref[...] = w_ref[...].astype(BF16)

    if w_rows_are_outputs:
        acc = lax.dot_general(a_ref[...], wb_ref[...], (((1,), (1,)), ((), ())),
                              preferred_element_type=F32)
    else:
        acc = jnp.dot(a_ref[...], wb_ref[...], preferred_element_type=F32)
    if epilogue == "relu2":
        acc = jnp.square(jnp.maximum(acc, 0.0))
    elif epilogue == "residual":
        acc = r_ref[...] + acc
    o_ref[...] = acc.astype(o_ref.dtype)


def _mm_tiles(t, k, n):
    if n % 128 != 0:
        return min(1024, t), n, 2
    tn = {2048: 1024, 4096: 512, 8192: 512}[k]
    tm = {2048: 1024, 4096: 512, 8192: 512}[k]
    return min(tm, t), min(tn, n), (1 if k == 8192 else 2)


def _matmul(a, w3, layer, col0, n, *, epilogue="none", residual=None, out_dtype=F32,
            w_rows_are_outputs=False):
    t, k = a.shape
    tm, tn, w_bufs = _mm_tiles(t, k, n)
    if w_rows_are_outputs:
        n_total = w3.shape[1]
        row0 = layer * n_total + col0
        w3 = w3.reshape(w3.shape[0] * n_total, k)
        w_spec = pl.BlockSpec((pl.Element(tn), pl.Element(k)), lambda j, i: (pl.multiple_of(row0 + j * tn, SUBLANES), 0),
                              pipeline_mode=pl.Buffered(w_bufs))
        wb_shape = (tn, k)
    else:
        cb0 = col0 // tn
        w_spec = pl.BlockSpec((None, k, tn), lambda j, i: (layer, 0, cb0 + j),
                              pipeline_mode=pl.Buffered(w_bufs))
        wb_shape = (k, tn)
    in_specs = [pl.BlockSpec((tm, k), lambda j, i: (i, 0)), w_spec]
    args = [a, w3]
    if epilogue == "residual":
        in_specs.append(pl.BlockSpec((tm, tn), lambda j, i: (i, j)))
        args.append(residual)
    return pl.pallas_call(
        functools.partial(_mm_kernel, epilogue=epilogue, w_rows_are_outputs=w_rows_are_outputs),
        grid=(n // tn, t // tm),
        in_specs=in_specs,
        out_specs=pl.BlockSpec((tm, tn), lambda j, i: (i, j)),
        out_shape=jax.ShapeDtypeStruct((t, n), out_dtype),
        scratch_shapes=[pltpu.VMEM(wb_shape, BF16)],
        compiler_params=_params(("arbitrary", "arbitrary")),
        name="matmul_" + epilogue,
    )(*args)


def _ssd_kernel(z_ref, xbc_ref, dt_ref, dtT2_ref, buf_ref, s0_ref, convw_ref, convb_ref,
                dtb_ref, dtb2_ref, alog_ref, alog2_ref, dskip_ref, normw_ref,
                y_ref, s_ref, bufout_ref, xpad_ref, *, c, valid):
    H, P, N, G = SSM_HEADS, SSM_HEAD_DIM, SSM_D_STATE, SSM_GROUPS
    W = 2 * P
    pairs_per_group = H // G // 2
    L = 2 * c
    log2c = int(math.log2(c))

    @pl.when(pl.program_id(1) == 0)
    def _():
        s_ref[...] = s0_ref[...]

    _conv_begin(xpad_ref, xbc_ref, buf_ref, c)
    rmask = _iota2((c, 1), 0) < valid

    def conv_silu(col0):
        t = _silu(_conv_tile(xpad_ref, convw_ref, c, col0, W) + convb_ref[:, col0:col0 + W])
        return jnp.where(rmask, t, 0.0) if valid < c else t

    row = _iota2((c, c), 0)
    col = _iota2((c, c), 1)
    ltri = (col <= row).astype(F32)
    rowp = _iota2((c, L), 0)
    lanep = _iota2((c, L), 1)
    tri2 = jnp.bitwise_and(lanep, c - 1) <= rowp
    rl = _iota2((L, L), 0)
    cl = _iota2((L, L), 1)
    utri2 = jnp.where(jnp.right_shift(rl, log2c) == jnp.right_shift(cl, log2c),
                      (rl <= cl).astype(F32), 0.0)
    first =_iota2((c, W), 1) < P
    first_row = _iota2((W, N), 0) < P

    dt = _softplus(dt_ref[...] + dtb_ref[...])
    la = dt * (-jnp.exp(alog_ref[...]))
    laT2 = _softplus(dtT2_ref[...] + dtb2_ref[...]) * (-jnp.exp(alog2_ref[...]))
    if valid < c:
        cmask = jnp.bitwise_and(_iota2((1, L), 1), c - 1) < valid
        la = jnp.where(rmask, la, 0.0)
        laT2 = jnp.where(cmask, laT2, 0.0)
        dt = jnp.where(rmask, dt, 0.0)
    acum = _sel_dot(ltri, la)
    acumT2 = _dot_sel(laT2, utri2)
    alast = acum[c - 1:c, :]
    elast = jnp.exp(alast)

    ch_head = jnp.right_shift(_iota2((H, SSM_D_INNER), 1), int(math.log2(P)))
    expand = (ch_head == _iota2((H, SSM_D_INNER), 0)).astype(F32)
    lane2 = _iota2((H, (H // 2) * L), 1)
    pk_head = 2 * jnp.right_shift(lane2, log2c + 1) + jnp.bitwise_and(jnp.right_shift(lane2, log2c), 1)
    expand2 = (pk_head == _iota2((H, (H // 2) * L), 0)).astype(F32)
    dt_x = _dot_sel(dt, expand)
    ea_x = jnp.exp(_dot_sel(acum, expand))
    ew_x = jnp.exp(_dot_sel(alast - acum, expand))
    dskip_x = _dot_sel(jnp.broadcast_to(dskip_ref[...], (SUBLANES, H)), expand)[0:1, :]
    acum_x2 = _dot_sel(acum, expand2)

    for g0 in range(0, G, SSD_WAVE):
        groups = range(g0, min(g0 + SSD_WAVE, G))
        prs = [g * pairs_per_group + j for g in groups for j in range(pairs_per_group)]
        bg, cg, cb2, xs, xdt, m2, xbd, xw = {}, {}, {}, {}, {}, {}, {}, {}
        for g in groups:
            bg[g] = conv_silu(SSM_D_INNER + g * N)
            cg[g] = conv_silu(SSM_D_INNER + G * N + g * N)
            cb2[g] = _dot_nt(cg[g], jnp.concatenate([bg[g], bg[g]], axis=0))
        for pr in prs:
            sl = slice(pr * W, (pr + 1) * W)
            xs[pr] = conv_silu(pr * W)
            xdt[pr] = xs[pr] * dt_x[:, sl]
            diff = acum_x2[:, pr * L:(pr + 1) * L] - acumT2[pr:pr + 1, :]
            decay = jnp.where(tri2, jnp.exp(jnp.where(tri2, diff, 0.0)), 0.0)
            m2[pr] = (cb2[pr // pairs_per_group] * decay).astype(BF16)
            xbd[pr] = jnp.concatenate([jnp.where(first, xdt[pr], 0.0),
                                       jnp.where(first, 0.0, xdt[pr])], axis=0).astype(BF16)
            xw[pr] = (xdt[pr] * ew_x[:, sl]).astype(BF16)
        y_intra, y_inter, upd = {}, {}, {}
        for pr in prs:
            g = pr // pairs_per_group
            sl = slice(pr * W, (pr + 1) * W)
            y_intra[pr] = _dot(m2[pr], xbd[pr])
            y_inter[pr] = _dot_nt(cg[g], s_ref[sl, :])
            upd[pr] = _dot_tn(xw[pr], bg[g])
        gated, sumsq = {}, {}
        for pr in prs:
            g = pr // pairs_per_group
            sl = slice(pr * W, (pr + 1) * W)
            y = y_intra[pr] + y_inter[pr] * ea_x[:, sl] + dskip_x[:, sl] * xs[pr]
            scale = jnp.where(first_row, elast[:, 2 * pr:2 * pr + 1], elast[:, 2 * pr + 1:2 * pr + 2])
            s_ref[sl, :] = scale * s_ref[sl, :] + upd[pr]
            y = y * _silu(z_ref[:, sl])
            ss = jnp.sum(y * y, axis=-1, keepdims=True)
            sumsq[g] = ss if g not in sumsq else sumsq[g] + ss
            gated[pr] = y
        for pr in prs:
            sl = slice(pr * W, (pr + 1) * W)
            rs = lax.rsqrt(sumsq[pr // pairs_per_group] * (1.0 / (pairs_per_group * W)) + EPS)
            y_ref[:, sl] = ((gated[pr] * rs) * normw_ref[:, sl]).astype(y_ref.dtype)
    _conv_carry(xpad_ref, bufout_ref, c, valid)


def _layer_state_io(s_all, layer, rows, cols):
    n_layers, b = s_all.shape[:2]
    spec = pl.BlockSpec((None, None, rows, cols), lambda bb, ii: (layer, bb, 0, 0))
    shape = jax.ShapeDtypeStruct((n_layers, b, rows, cols), F32)
    return s_all.reshape(n_layers, b, rows, cols), spec, shape


def _continued_outputs(n_in, accs):
    args, aliases = [], {}
    for out_idx, acc in enumerate(accs):
        if acc is not None:
            aliases[n_in + len(args)] = out_idx
            args.append(acc)
    return args, [pl.BlockSpec(memory_space=pl.ANY)] * len(args), aliases


def _without_refs(kern, pos, count):
    if count == 0:
        return kern
    return lambda *refs: kern(*refs[:pos], *refs[pos + count:])


def _ssd(z, xbc, dt, buf, s_all, y_acc, s_acc, p, i, *, b, n, c, valid, row0):
    t_all, t, blk0 = z.shape[0], b * n * c, row0 // c
    H, P, N = SSM_HEADS, SSM_HEAD_DIM, SSM_D_STATE
    C = SSM_CONV_DIM
    tok = lambda w: pl.BlockSpec((c, w), lambda bb, ii: (blk0 + bb * n + ii, 0))
    full2 = lambda a: pl.BlockSpec(a.shape, lambda bb, ii: (0, 0))
    dtT2 = _pair_packed_transpose(dt[row0:row0 + t], b, n, c)
    convw = p["ssm_conv_w"][i]
    convb = p["ssm_conv_b"][i].reshape(1, C)
    dtb = p["ssm_dt_bias"][i].reshape(1, H)
    dtb2 = jnp.repeat(p["ssm_dt_bias"][i].reshape(H // 2, 2), c, axis=1)
    alog = p["ssm_A_log"][i].reshape(1, H)
    alog2 = jnp.repeat(p["ssm_A_log"][i].reshape(H // 2, 2), c, axis=1)
    dskip = p["ssm_D"][i].reshape(1, H)
    normw = p["ssm_norm_w"][i].reshape(1, SSM_D_INNER)
    small = [convw, convb, dtb, dtb2, alog, alog2, dskip, normw]
    s_in, s_spec, s_shape = _layer_state_io(s_all, i, H * P, N)
    n_in = 6 + len(small)
    acc_args, acc_specs, aliases = _continued_outputs(n_in, [y_acc, s_acc, None])
    y, s_new, buf_new = pl.pallas_call(
        _without_refs(functools.partial(_ssd_kernel, c=c, valid=valid), n_in, len(acc_args)),
        grid=(b, n),
        in_specs=[tok(SSM_D_INNER), tok(C), tok(H),
                  pl.BlockSpec((None, None, H // 2, 2 * c), lambda bb, ii: (bb, ii, 0, 0)),
                  pl.BlockSpec((None, CONV_W - 1, C), lambda bb, ii: (bb, 0, 0)),
                  s_spec]
                 + [full2(a) for a in small] + acc_specs,
        out_specs=[tok(SSM_D_INNER), s_spec,
                   pl.BlockSpec((None, CONV_W - 1, C), lambda bb, ii: (bb, 0, 0))],
        out_shape=[jax.ShapeDtypeStruct((t_all, SSM_D_INNER), BF16), s_shape,
                   jax.ShapeDtypeStruct((b, CONV_W - 1, C), F32)],
        scratch_shapes=[pltpu.VMEM((8 + c, C), F32)],
        input_output_aliases=aliases,
        compiler_params=_params(("arbitrary", "arbitrary")),
        name="ssd_scan",
    )(z, xbc, dt, dtT2, buf, s_in, *small, *acc_args)
    return y, s_new, buf_new


def _gla_kernel(hq_ref, hf_ref, hi_ref, hg_ref, s0_ref, lb_ref, normw_ref,
                o_ref, s_ref, *, c, valid, layer):
    H, K = HG_HEADS, HG_HEAD_DIM

    @pl.when(pl.program_id(1) == 0)
    def _():
        s_ref[...] = s0_ref[...]

    lbp = lb_ref[...]
    mx = lbp[0:1, :]
    for j in range(1, N_AB):
        mx = jnp.maximum(mx, lbp[j:j + 1, :])
    e = [jnp.exp(lbp[j:j + 1, :] - mx) for j in range(N_AB)]
    den = e[0]
    for j in range(1, N_AB):
        den = den + e[j]
    lb = e[0] / den
    for j in range(1, layer + 1):
        lb = lb + e[j] / den
    lb = lb - e[0] / den

    rmask = _iota2((c, 1), 0) < valid
    one_m_lb = 1.0 - lb

    def masked(t):
        return jnp.where(rmask, t, 0.0) if valid < c else t

    logf, ks = [], []
    for h in range(H):
        sl = slice(h * K, (h + 1) * K)
        sig = _sigmoid(hf_ref[:, sl])
        logf.append(masked(jnp.log(jnp.maximum(lb[:, sl] + one_m_lb[:, sl] * sig, TINY))))
        ks.append(masked(one_m_lb[:, sl] * (1.0 - sig)))

    row = _iota2((c, c), 0)
    col = _iota2((c, c), 1)
    ltri = (col <= row).astype(F32)
    g_all = _sel_dot(ltri, jnp.concatenate(logf, axis=1))

    rowv = _iota2((c, 1), 0)
    eye = row == col
    levels = []
    m = c // 2
    while m >= 1:
        sh2m = int(math.log2(2 * m))
        t_up = jnp.bitwise_and(row, 2 * m - 1) - m
        s_lo = (m - 1) - jnp.bitwise_and(col, 2 * m - 1)
        other_blk = -jnp.bitwise_xor(jnp.right_shift(row, sh2m), jnp.right_shift(col, sh2m))
        levels.append((m, jnp.bitwise_and(rowv, 2 * m - 1) >= m,
                       jnp.bitwise_or(jnp.bitwise_or(t_up, s_lo), other_blk) >= 0))
        m //= 2

    for h0 in range(0, H, GLA_WAVE):
        heads = range(h0, min(h0 + GLA_WAVE, H))
        att, qs, vs = {}, {}, {}
        for h in heads:
            sl = slice(h * K, (h + 1) * K)
            g = g_all[:, sl]
            q = masked(_silu(hq_ref[:, sl]))
            k = ks[h]
            a = jnp.where(eye, _dot_nt(q, k), 0.0)
            for m, upper, pair_mask in levels:
                d = g - _block_mid_rows(g, m, rowv)
                zl = (jnp.where(upper, q, k) * jnp.exp(jnp.where(upper, d, -d))).astype(BF16)
                a = a + jnp.where(pair_mask, _dot_nt(zl, zl), 0.0)
            att[h], qs[h], vs[h] = a, q, masked(hi_ref[:, sl]).astype(BF16)
        for h in heads:
            sl = slice(h * K, (h + 1) * K)
            g = g_all[:, sl]
            glast = g[c - 1:c, :]
            sh = s_ref[sl, :]
            y = _dot(att[h], vs[h]) + _dot(qs[h] * jnp.exp(g), sh)
            dec_col = jnp.exp(jnp.broadcast_to(glast, (SUBLANES, K)).T[:, 0:1])
            s_ref[sl, :] = dec_col * sh + _dot_tn(ks[h] * jnp.exp(glast - g), vs[h])
            yn = y * lax.rsqrt(jnp.mean(y * y, axis=-1, keepdims=True) + EPS)
            o_ref[:, sl] = (yn * normw_ref[:, sl] * _silu(hg_ref[:, sl])).astype(o_ref.dtype)


def _gla(hh, s_all, o_acc, s_acc, p, i, *, b, n, c, valid, row0):
    t_all, blk0 = hh.shape[0], row0 // c
    H, K = HG_HEADS, HG_HEAD_DIM
    W = HG_WIDTH
    tok = lambda j: pl.BlockSpec((c, W), lambda bb, ii: (blk0 + bb * n + ii, j))
    full2 = lambda a: pl.BlockSpec(a.shape, lambda bb, ii: (0, 0))
    lb = p["hg_lower_bounds"]
    normw = p["hg_norm_w"][i].reshape(1, W)
    s_in, s_spec, s_shape = _layer_state_io(s_all, i, H * K, K)
    n_in = 7
    acc_args, acc_specs, aliases = _continued_outputs(n_in, [o_acc, s_acc])
    o, s_new = pl.pallas_call(
        _without_refs(functools.partial(_gla_kernel, c=c, valid=valid, layer=i), n_in, len(acc_args)),
        grid=(b, n),
        in_specs=[tok(0), tok(1), tok(2), tok(3), s_spec, full2(lb), full2(normw)] + acc_specs,
        out_specs=[tok(0), s_spec],
        out_shape=[jax.ShapeDtypeStruct((t_all, W), BF16), s_shape],
        input_output_aliases=aliases,
        compiler_params=_params(("arbitrary", "arbitrary")),
        name="gla_scan",
    )(hh, hh, hh, hh, s_in, lb, normw, *acc_args)
    return o, s_new


def _split_bf16(x):
    hi = x.astype(BF16)
    return hi, (x - hi.astype(F32)).astype(BF16)


def _gdn_kernel(qkv_ref, z_ref, ba_ref, aT2_ref, buf_ref, s0_ref, convw_ref,
                dtb_ref, dtb2_ref, alog_ref, alog2_ref, normw_ref,
                o_ref, s_ref, bufout_ref, xpad_ref, *, c, valid, wave):
    HV, HK, K, V = GDN_V_HEADS, GDN_K_HEADS, GDN_HEAD_K, GDN_HEAD_V
    L = 2 * c
    log2c = int(math.log2(c))

    @pl.when(pl.program_id(1) == 0)
    def _():
        s_ref[...] = s0_ref[...]

    _conv_begin(xpad_ref, qkv_ref, buf_ref, c)
    rmask = _iota2((c, 1), 0) < valid

    def conv_silu(col0):
        t = _silu(_conv_tile(xpad_ref, convw_ref, c, col0, K))
        return jnp.where(rmask, t, 0.0) if valid < c else t

    row = _iota2((c, c), 0)
    col = _iota2((c, c), 1)
    ltri = (col <= row).astype(F32)
    rowp = _iota2((c, L), 0)
    lanep = _iota2((c, L), 1)
    left = lanep < c
    sp = jnp.bitwise_and(lanep, c - 1)
    tri2 = sp <= rowp
    strict2 = sp < rowp
    eye2 = (sp == rowp).astype(F32)
    rl = _iota2((L, L), 0)
    cl = _iota2((L, L), 1)
    utri2 = jnp.where(jnp.right_shift(rl, log2c) == jnp.right_shift(cl, log2c),
                      (rl <= cl).astype(F32), 0.0)

    ba = ba_ref[...]
    beta = _sigmoid(ba[:, :HV])
    gl = -jnp.exp(alog_ref[...]) * _softplus(ba[:, HV:] + dtb_ref[...])
    glT2 = -jnp.exp(alog2_ref[...]) * _softplus(aT2_ref[...] + dtb2_ref[...])
    if valid < c:
        cmask = jnp.bitwise_and(_iota2((1, L), 1), c - 1) < valid
        beta = jnp.where(rmask, beta, 0.0)
        gl = jnp.where(rmask, gl, 0.0)
        glT2 = jnp.where(cmask, glT2, 0.0)
    gcum = _sel_dot(ltri, gl)
    gcumT2 = _dot_sel(glT2, utri2)
    egc = jnp.exp(gcum)
    glast = gcum[c - 1:c, :]
    elast = jnp.exp(glast)
    edec = jnp.exp(glast - gcum)

    n_sq = log2c - 1
    normw = normw_ref[...]
    zeros_cv = jnp.zeros((c, V), F32)

    def pair_cols(a, hk):
        return jnp.where(left, a[:, 2 * hk:2 * hk + 1], a[:, 2 * hk + 1:2 * hk + 2])

    def block_diag(x2):
        return jnp.concatenate([jnp.where(left, x2, 0.0), jnp.where(left, 0.0, x2)], axis=0)

    def pair_product(lhs_split, x2):
        lh, ll = lhs_split
        rh, rlo = _split_bf16(block_diag(x2))
        return (jnp.dot(lh, rh, preferred_element_type=F32)
                + jnp.dot(ll, rh, preferred_element_type=F32)
                + jnp.dot(lh, rlo, preferred_element_type=F32))

    for w0 in range(0, HK, wave):
        hks = list(range(w0, min(w0 + wave, HK)))
        qn, kn, gam, qkg, pw, tm = {}, {}, {}, {}, {}, {}
        for hk in hks:
            qh = conv_silu(hk * K)
            kh = conv_silu(GDN_KEY_DIM + hk * K)
            qn[hk] = qh * lax.rsqrt(jnp.sum(qh * qh, axis=-1, keepdims=True) + EPS) * (K ** -0.5)
            kn[hk] = kh * lax.rsqrt(jnp.sum(kh * kh, axis=-1, keepdims=True) + EPS)
            gq = _dot_nt(jnp.concatenate([kn[hk], qn[hk]], axis=0),
                         jnp.concatenate([kn[hk], kn[hk]], axis=0))
            diff = pair_cols(gcum, hk) - gcumT2[hk:hk + 1, :]
            gam[hk] = jnp.where(tri2, jnp.exp(jnp.where(tri2, diff, 0.0)), 0.0)
            pw[hk] = -jnp.where(strict2, pair_cols(beta, hk) * gq[:c] * gam[hk], 0.0)
            qkg[hk] = gq[c:] * gam[hk]
            tm[hk] = eye2 + pw[hk]
        for _ in range(n_sq):
            for hk in hks:
                pw[hk] = pair_product(_split_bf16(pw[hk]), pw[hk])
            for hk in hks:
                tm[hk] = tm[hk] + pair_product(_split_bf16(tm[hk]), pw[hk])
        wu, ws_qs, v_new = {}, {}, {}
        for hk in hks:
            rhs = []
            for j in range(2):
                h = 2 * hk + j
                bh = beta[:, h:h + 1]
                vh = conv_silu(2 * GDN_KEY_DIM + h * V)
                rhs.append(jnp.concatenate([kn[hk] * (bh * egc[:, h:h + 1]), vh * bh], axis=1))
            wu[hk] = _dot(block_diag(tm[hk]), jnp.concatenate(rhs, axis=0))
        for hk in hks:
            for j in range(2):
                h = 2 * hk + j
                wq = jnp.concatenate([wu[hk][j * c:(j + 1) * c, :K], qn[hk] * egc[:, h:h + 1]], axis=0)
                ws_qs[h] = _dot(wq, s_ref[h * K:(h + 1) * K, :])
                v_new[h] = wu[hk][j * c:(j + 1) * c, K:] - ws_qs[h][:c]
        for hk in hks:
            ha, hb = 2 * hk, 2 * hk + 1
            y2 = _dot(block_diag(qkg[hk]), jnp.concatenate([v_new[ha], v_new[hb]], axis=0))
            kdec = jnp.concatenate([kn[hk] * edec[:, ha:ha + 1], kn[hk] * edec[:, hb:hb + 1]], axis=0)
            vbd = jnp.concatenate([jnp.concatenate([v_new[ha], zeros_cv], axis=1),
                                   jnp.concatenate([zeros_cv, v_new[hb]], axis=1)], axis=0)
            upd = _dot_tn(kdec, vbd)
            for j, h in enumerate((ha, hb)):
                sl = slice(h * K, (h + 1) * K)
                s_ref[sl, :] = elast[:, h:h + 1] * s_ref[sl, :] + upd[:, j * V:(j + 1) * V]
                y = ws_qs[h][c:] + y2[j * c:(j + 1) * c]
                yn = (y * lax.rsqrt(jnp.mean(y * y, axis=-1, keepdims=True) + EPS)) * normw
                o_ref[:, h * V:(h + 1) * V] = (yn * _silu(z_ref[:, h * V:(h + 1) * V])).astype(o_ref.dtype)
    _conv_carry(xpad_ref, bufout_ref, c, valid)


def _pair_packed_transpose(a, b, n, c):
    hh = a.shape[-1] // 2
    x = a.reshape(b, n, c, hh, 2)
    return jnp.transpose(x, (0, 1, 3, 4, 2)).reshape(b, n, hh, 2 * c)


def _gdn(qkv, z, ba, buf, s_all, o_acc, s_acc, p, i, *, b, n, c, valid, row0):
    t_all, t, blk0 = z.shape[0], b * n * c, row0 // c
    HV, HK, K, V = GDN_V_HEADS, GDN_K_HEADS, GDN_HEAD_K, GDN_HEAD_V
    C = GDN_CONV_DIM
    tok = lambda w: pl.BlockSpec((c, w), lambda bb, ii: (blk0 + bb * n + ii, 0))
    full2 = lambda a: pl.BlockSpec(a.shape, lambda bb, ii: (0, 0))
    aT2 = _pair_packed_transpose(ba[row0:row0 + t, HV:], b, n, c)
    convw = p["gdn_conv_w"][i]
    dtb = p["gdn_dt_bias"][i].reshape(1, HV)
    dtb2 = jnp.repeat(p["gdn_dt_bias"][i].reshape(HK, 2), c, axis=1)
    alog = p["gdn_A_log"][i].reshape(1, HV)
    alog2 = jnp.repeat(p["gdn_A_log"][i].reshape(HK, 2), c, axis=1)
    normw = p["gdn_norm_w"][i].reshape(1, V)
    small = [convw, dtb, dtb2, alog, alog2, normw]
    s_in, s_spec, s_shape = _layer_state_io(s_all, i, HV * K, V)
    n_in = 6 + len(small)
    acc_args, acc_specs, aliases = _continued_outputs(n_in, [o_acc, s_acc, None])
    o, s_new, buf_new = pl.pallas_call(
        _without_refs(functools.partial(_gdn_kernel, c=c, valid=valid, wave=GDN_WAVE), n_in, len(acc_args)),
        grid=(b, n),
        in_specs=[tok(C), tok(GDN_VAL_DIM), tok(2 * HV),
                  pl.BlockSpec((None, None, HK, 2 * c), lambda bb, ii: (bb, ii, 0, 0)),
                  pl.BlockSpec((None, CONV_W - 1, C), lambda bb, ii: (bb, 0, 0)),
                  s_spec]
                 + [full2(a) for a in small] + acc_specs,
        out_specs=[tok(GDN_VAL_DIM), s_spec,
                   pl.BlockSpec((None, CONV_W - 1, C), lambda bb, ii: (bb, 0, 0))],
        out_shape=[jax.ShapeDtypeStruct((t_all, GDN_VAL_DIM), BF16), s_shape,
                   jax.ShapeDtypeStruct((b, CONV_W - 1, C), F32)],
        scratch_shapes=[pltpu.VMEM((8 + c, C), F32)],
        input_output_aliases=aliases,
        compiler_params=_params(("arbitrary", "arbitrary")),
        name="gdn_scan",
    )(qkv, z, ba, aT2, buf, s_in, *small, *acc_args)
    return o, s_new, buf_new


def _trunk(groups, p):
    d = groups[0]["x"].shape[-1]
    row0 = 0
    for gr in groups:
        b, lp, _ = gr["x"].shape
        gr["dims"] = dict(b=b, n=lp // gr["c"], c=gr["c"], valid=gr["valid"], row0=row0)
        gr["rows"] = b * lp
        gr["new"] = dict(ssm=None, hg=None, gdn=None, ssm_conv=[], gdn_conv=[])
        row0 += b * lp
    xt = jnp.concatenate([gr["x"].reshape(gr["rows"], d) for gr in groups], axis=0)
    for l in range(DEPTH):
        i = l // 2
        h = _rmsnorm(xt, p["norm_mix"][l], BF16)
        if l % 2 == 0:
            in_proj = functools.partial(_matmul, h, p["w_in_ab_t"], i, w_rows_are_outputs=True)
            z = in_proj(0, SSM_D_INNER)
            xbc = in_proj(AB_OFF_XBC, SSM_CONV_DIM)
            dt = in_proj(AB_OFF_DT, SSM_HEADS)
            hh = in_proj(AB_OFF_HG, 4 * HG_WIDTH)
            y, o = None, None
            for gr in groups:
                new = gr["new"]
                y, new["ssm"], b1 = _ssd(z, xbc, dt, gr["ssm_conv"][i], gr["ssm"], y, new["ssm"],
                                         p, i, **gr["dims"])
                o, new["hg"] = _gla(hh, gr["hg"], o, new["hg"], p, i, **gr["dims"])
                new["ssm_conv"].append(b1)
            mix = jnp.concatenate([y, o], axis=-1)
            xt = _matmul(mix, p["w_out_ab"], i, 0, D_MODEL, epilogue="residual", residual=xt)
        else:
            in_proj = functools.partial(_matmul, h, p["w_in_c_t"], i, w_rows_are_outputs=True)
            qkv = in_proj(0, GDN_CONV_DIM)
            z = in_proj(C_OFF_Z, GDN_VAL_DIM)
            ba = in_proj(C_OFF_BA, 2 * GDN_V_HEADS)
            o = None
            for gr in groups:
                new = gr["new"]
                o, new["gdn"], b3 = _gdn(qkv, z, ba, gr["gdn_conv"][i], gr["gdn"], o, new["gdn"],
                                         p, i, **gr["dims"])
                new["gdn_conv"].append(b3)
            xt = _matmul(o, p["w_out_c"], i, 0, D_MODEL, epilogue="residual", residual=xt)
        h = _rmsnorm(xt, p["norm_mlp"][l], BF16)
        u = _matmul(h, p["w_ff1"], l, 0, FFN_HIDDEN, epilogue="relu2", out_dtype=BF16)
        xt = _matmul(u, p["w_ff2"], l, 0, D_MODEL, epilogue="residual", residual=xt)
    outs = []
    for gr in groups:
        new = gr["new"]
        y = _rmsnorm(xt, p["norm_final"], F32, gr["dims"]["row0"], gr["rows"]).reshape(gr["x"].shape)
        outs.append((y, new["ssm"].reshape(gr["ssm"].shape), jnp.stack(new["ssm_conv"]),
                     new["hg"].reshape(gr["hg"].shape), new["gdn"].reshape(gr["gdn"].shape),
                     jnp.stack(new["gdn_conv"])))
    return outs


def kernel(x_prompt, x_sample, state_ssm, state_ssm_conv, state_hgrn, state_gdn, state_gdn_conv, norm_mix, norm_mlp, norm_final, w_in_ab, ssm_conv_w, ssm_conv_b, ssm_dt_bias, ssm_A_log, ssm_D, ssm_norm_w, hg_lower_bounds, hg_norm_w, w_out_ab, w_in_c, gdn_conv_w, gdn_dt_bias, gdn_A_log, gdn_norm_w, w_out_c, w_ff1, w_ff2):
    p = dict(norm_mix=norm_mix, norm_mlp=norm_mlp, norm_final=norm_final, w_in_ab=w_in_ab,
             ssm_conv_w=ssm_conv_w, ssm_conv_b=ssm_conv_b, ssm_dt_bias=ssm_dt_bias, ssm_A_log=ssm_A_log,
             ssm_D=ssm_D, ssm_norm_w=ssm_norm_w, hg_lower_bounds=hg_lower_bounds, hg_norm_w=hg_norm_w,
             w_out_ab=w_out_ab, w_in_c=w_in_c, gdn_conv_w=gdn_conv_w, gdn_dt_bias=gdn_dt_bias,
             gdn_A_log=gdn_A_log, gdn_norm_w=gdn_norm_w, w_out_c=w_out_c, w_ff1=w_ff1, w_ff2=w_ff2)
    p["w_in_ab_t"] = jnp.swapaxes(w_in_ab, 1, 2)
    p["w_in_c_t"] = jnp.swapaxes(w_in_c, 1, 2)

    bp, seq, _ = x_prompt.shape
    z_ssm = jnp.zeros((N_AB, bp, SSM_HEADS, SSM_HEAD_DIM, SSM_D_STATE), F32)
    z_ssm_conv = jnp.zeros((N_AB, bp, CONV_W - 1, SSM_CONV_DIM), F32)
    z_hg = jnp.zeros((N_AB, bp, HG_HEADS, HG_HEAD_DIM, HG_HEAD_DIM), F32)
    z_gdn = jnp.zeros((N_C, bp, GDN_V_HEADS, GDN_HEAD_K, GDN_HEAD_V), F32)
    z_gdn_conv = jnp.zeros((N_C, bp, CONV_W - 1, GDN_CONV_DIM), F32)
    c_p = min(CHUNK, seq)
    prompt = dict(x=x_prompt, c=c_p, valid=c_p, ssm=z_ssm, ssm_conv=z_ssm_conv, hg=z_hg, gdn=z_gdn,
                  gdn_conv=z_gdn_conv)

    bs, dec, _ = x_sample.shape
    c_s = -(-dec // SUBLANES) * SUBLANES
    xs_pad = jnp.pad(x_sample, ((0, 0), (0, c_s - dec), (0, 0)))
    sample = dict(x=xs_pad, c=c_s, valid=dec, ssm=state_ssm, ssm_conv=state_ssm_conv, hg=state_hgrn,
                  gdn=state_gdn, gdn_conv=state_gdn_conv)
    outs_p, outs_s = _trunk([prompt, sample], p)
    y_sample = outs_s[0][:, :dec]
    return (outs_p[0], y_sample) + tuple(outs_p[1:]) + tuple(outs_s[1:])
```

```python
import functools
import math

import jax
import jax.numpy as jnp
from jax import lax
from jax.experimental import pallas as pl
from jax.experimental.pallas import tpu as pltpu

F32 = jnp.float32
BF16 = jnp.bfloat16
HIGHEST = lax.Precision.HIGHEST

D_MODEL = 2048
DEPTH = 4
N_AB = (DEPTH + 1) // 2
N_C = DEPTH // 2
CONV_W = 4
CHUNK = 64
EPS = 1e-6
TINY = 1e-30

SSM_D_INNER = D_MODEL
SSM_HEAD_DIM = 64
SSM_HEADS = SSM_D_INNER // SSM_HEAD_DIM
SSM_GROUPS = 4
SSM_D_STATE = 128
SSM_CONV_DIM = SSM_D_INNER + 2 * SSM_GROUPS * SSM_D_STATE

HG_WIDTH = D_MODEL
HG_HEAD_DIM = 128
HG_HEADS = HG_WIDTH // HG_HEAD_DIM

GDN_HEAD_K = 128
GDN_HEAD_V = 128
GDN_K_HEADS = D_MODEL // GDN_HEAD_K
GDN_V_HEADS = 2 * GDN_K_HEADS
GDN_KEY_DIM = GDN_K_HEADS * GDN_HEAD_K
GDN_VAL_DIM = GDN_V_HEADS * GDN_HEAD_V
GDN_CONV_DIM = 2 * GDN_KEY_DIM + GDN_VAL_DIM

FFN_HIDDEN = 4 * D_MODEL

AB_OFF_XBC = SSM_D_INNER
AB_OFF_DT = AB_OFF_XBC + SSM_CONV_DIM
AB_OFF_HG = AB_OFF_DT + SSM_HEADS
C_OFF_Z = GDN_CONV_DIM
C_OFF_BA = C_OFF_Z + GDN_VAL_DIM

SUBLANES = 8
VMEM_LIMIT = 52 * 1024 * 1024
GDN_WAVE = 8
GDN_WAVE_SHORT_CHUNK = 16
GLA_WAVE = 16
SSD_WAVE = 4


def _params(sem):
    return pltpu.CompilerParams(dimension_semantics=sem, vmem_limit_bytes=VMEM_LIMIT)


def _dot(a, b):
    return jnp.dot(a.astype(BF16), b.astype(BF16), preferred_element_type=F32)


def _dot_nt(a, b):
    return lax.dot_general(a.astype(BF16), b.astype(BF16), (((1,), (1,)), ((), ())),
                           preferred_element_type=F32)


def _dot_tn(a, b):
    return lax.dot_general(a.astype(BF16), b.astype(BF16), (((0,), (0,)), ((), ())),
                           preferred_element_type=F32)


def _split3_bf16(x):
    p0 = x.astype(BF16)
    r1 = x - p0.astype(F32)
    p1 = r1.astype(BF16)
    return p0, p1, (r1 - p1.astype(F32)).astype(BF16)


def _sel_dot(sel, x):
    s = sel.astype(BF16)
    p0, p1, p2 = _split3_bf16(x)
    return (jnp.dot(s, p0, preferred_element_type=F32) + jnp.dot(s, p1, preferred_element_type=F32)
            + jnp.dot(s, p2, preferred_element_type=F32))


def _dot_sel(x, sel):
    s = sel.astype(BF16)
    p0, p1, p2 = _split3_bf16(x)
    return (jnp.dot(p0, s, preferred_element_type=F32) + jnp.dot(p1, s, preferred_element_type=F32)
            + jnp.dot(p2, s, preferred_element_type=F32))


def _sigmoid(x):
    return 1.0 / (1.0 + jnp.exp(-x))


def _silu(x):
    return x * _sigmoid(x)


def _softplus(x):
    return jnp.maximum(x, 0.0) + jnp.log1p(jnp.exp(-jnp.abs(x)))


def _iota2(shape, dim):
    return lax.broadcasted_iota(jnp.int32, shape, dim)


def _block_mid_rows(g, m, rowv):
    c, w = g.shape
    if 2 * m >= SUBLANES:
        return jnp.concatenate([jnp.broadcast_to(g[b0 + m:b0 + m + 1, :], (2 * m, w))
                                for b0 in range(0, c, 2 * m)], axis=0)
    pos = jnp.bitwise_and(rowv, 2 * m - 1)
    out = g
    for off in range(1 - m, m + 1):
        if off != 0:
            out = jnp.where(pos == m - off, pltpu.roll(g, (-off) % c, 0), out)
    return out


def _conv_begin(xpad_ref, x_ref, buf_ref, c):
    @pl.when(pl.program_id(1) == 0)
    def _():
        xpad_ref[5:8, :] = buf_ref[...]

    xpad_ref[8:8 + c, :] = x_ref[...]


def _conv_tile(xpad_ref, w_ref, c, col0, width):
    sl = slice(col0, col0 + width)
    y = xpad_ref[5:5 + c, sl] * w_ref[0:1, sl]
    for j in range(1, CONV_W):
        y = y + xpad_ref[5 + j:5 + j + c, sl] * w_ref[j:j + 1, sl]
    return y


def _conv_carry(xpad_ref, bufout_ref, c, valid_last):
    @pl.when(pl.program_id(1) == pl.num_programs(1) - 1)
    def _():
        bufout_ref[...] = xpad_ref[8 + valid_last - 3:8 + valid_last, :]

    xpad_ref[5:8, :] = xpad_ref[8 + c - 3:8 + c, :]


def _rmsnorm_kernel(x_ref, w_ref, o_ref):
    x = x_ref[...]
    ms = jnp.mean(x * x, axis=-1, keepdims=True)
    o_ref[...] = ((x * lax.rsqrt(ms + EPS)) * w_ref[...]).astype(o_ref.dtype)


def _rmsnorm(x, w, out_dtype, row0=0, rows=None):
    d = x.shape[1]
    t = x.shape[0] if rows is None else rows
    tm = min(512, t)
    blk0 = row0 // tm
    return pl.pallas_call(
        _rmsnorm_kernel,
        grid=(t // tm,),
        in_specs=[pl.BlockSpec((tm, d), lambda i: (blk0 + i, 0)),
                  pl.BlockSpec((1, d), lambda i: (0, 0))],
        out_specs=pl.BlockSpec((tm, d), lambda i: (i, 0)),
        out_shape=jax.ShapeDtypeStruct((t, d), out_dtype),
        compiler_params=_params(("arbitrary",)),
        name="rmsnorm",
    )(x, w.reshape(1, d))


def _mm_kernel(*refs, n_seg, group_tiles, epilogue, w_rows_are_outputs):
    n_a = n_seg * len(group_tiles)
    a_refs, refs = refs[:n_a], refs[n_a:]
    if epilogue == "residual":
        w_ref, r_ref, o_ref, wb_ref = refs
    else:
        w_ref, o_ref, wb_ref = refs
    i = pl.program_id(1)

    @pl.when(i == 0)
    def _():
        wb_ref[...] = w_ref[...].astype(BF16)

    def row_tile(seg_refs):
        if w_rows_are_outputs:
            acc = lax.dot_general(seg_refs[0][...], wb_ref[...], (((1,), (1,)), ((), ())),
                                  preferred_element_type=F32)
        else:
            acc, k0 = None, 0
            for a_ref in seg_refs:
                k1 = k0 + a_ref.shape[1]
                part = jnp.dot(a_ref[...], wb_ref[k0:k1, :], preferred_element_type=F32)
                acc, k0 = (part if acc is None else acc + part), k1
        if epilogue == "relu2":
            acc = jnp.square(jnp.maximum(acc, 0.0))
        elif epilogue == "residual":
            acc = r_ref[...] + acc
        o_ref[...] = acc.astype(o_ref.dtype)

    tile0 = 0
    for g, n_tiles in enumerate(group_tiles):
        seg_refs = a_refs[g * n_seg:(g + 1) * n_seg]
        if len(group_tiles) == 1:
            row_tile(seg_refs)
        else:
            pl.when(jnp.logical_and(i >= tile0, i < tile0 + n_tiles))(
                functools.partial(row_tile, seg_refs))
        tile0 += n_tiles


def _mm_tiles(t, k, n):
    if n % 128 != 0:
        return 1024, n, 2
    tn = {2048: 1024, 4096: 512, 8192: 512}[k]
    tm = {2048: 1024, 4096: 1024, 8192: 512}[k]
    return tm, min(tn, n), (2 if k == 2048 else 1)


def _matmul(a, w3, layer, col0, n, *, epilogue="none", residual=None, out_dtype=F32,
            w_rows_are_outputs=False):
    a_segs = a if isinstance(a, (tuple, list)) else (a,)
    a_segs = [seg if isinstance(seg, (tuple, list)) else (seg,) for seg in a_segs]
    group_rows = [arr.shape[0] for arr in a_segs[0]]
    t, k = sum(group_rows), sum(seg[0].shape[1] for seg in a_segs)
    tm, tn, w_bufs = _mm_tiles(t, k, n)
    while any(r % tm for r in group_rows):
        tm //= 2
    group_tiles = tuple(r // tm for r in group_rows)
    tile0 = [sum(group_tiles[:g]) for g in range(len(group_tiles))]

    def a_spec(g, width):
        return pl.BlockSpec((tm, width),
                            lambda j, i: (jnp.clip(i - tile0[g], 0, group_tiles[g] - 1), 0))
    if w_rows_are_outputs:
        n_total = w3.shape[1]
        row0 = layer * n_total + col0
        w3 = w3.reshape(w3.shape[0] * n_total, k)
        w_spec = pl.BlockSpec((pl.Element(tn), pl.Element(k)), lambda j, i: (pl.multiple_of(row0 + j * tn, SUBLANES), 0),
                              pipeline_mode=pl.Buffered(w_bufs))
        wb_shape = (tn, k)
    else:
        cb0 = col0 // tn
        w_spec = pl.BlockSpec((None, k, tn), lambda j, i: (layer, 0, cb0 + j),
                              pipeline_mode=pl.Buffered(w_bufs))
        wb_shape = (k, tn)
    in_specs, args = [], []
    for g in range(len(group_tiles)):
        for seg in a_segs:
            in_specs.append(a_spec(g, seg[g].shape[1]))
            args.append(seg[g])
    in_specs.append(w_spec)
    args.append(w3)
    if epilogue == "residual":
        in_specs.append(pl.BlockSpec((tm, tn), lambda j, i: (i, j)))
        args.append(residual)
    return pl.pallas_call(
        functools.partial(_mm_kernel, n_seg=len(a_segs), group_tiles=group_tiles, epilogue=epilogue,
                          w_rows_are_outputs=w_rows_are_outputs),
        grid=(n // tn, t // tm),
        in_specs=in_specs,
        out_specs=pl.BlockSpec((tm, tn), lambda j, i: (i, j)),
        out_shape=jax.ShapeDtypeStruct((t, n), out_dtype),
        scratch_shapes=[pltpu.VMEM(wb_shape, BF16)],
        compiler_params=_params(("arbitrary", "arbitrary")),
        name="matmul_" + epilogue,
    )(*args)


def _ssd_kernel(z_ref, xbc_ref, dt_ref, dtT2_ref, buf_ref, s0_ref, convw_ref, convb_ref,
                dtb_ref, dtb2_ref, alog_ref, alog2_ref, dskip_ref, normw_ref,
                y_ref, s_ref, bufout_ref, xpad_ref, *, c, valid):
    H, P, N, G = SSM_HEADS, SSM_HEAD_DIM, SSM_D_STATE, SSM_GROUPS
    W = 2 * P
    pairs_per_group = H // G // 2
    L = 2 * c
    log2c = int(math.log2(c))

    @pl.when(pl.program_id(1) == 0)
    def _():
        s_ref[...] = s0_ref[...]

    _conv_begin(xpad_ref, xbc_ref, buf_ref, c)
    rmask = _iota2((c, 1), 0) < valid

    def conv_silu(col0):
        t = _silu(_conv_tile(xpad_ref, convw_ref, c, col0, W) + convb_ref[:, col0:col0 + W])
        return jnp.where(rmask, t, 0.0) if valid < c else t

    row = _iota2((c, c), 0)
    col = _iota2((c, c), 1)
    ltri = (col <= row).astype(F32)
    rowp = _iota2((c, L), 0)
    lanep = _iota2((c, L), 1)
    tri2 = jnp.bitwise_and(lanep, c - 1) <= rowp
    rl = _iota2((L, L), 0)
    cl = _iota2((L, L), 1)
    utri2 = jnp.where(jnp.right_shift(rl, log2c) == jnp.right_shift(cl, log2c),
                      (rl <= cl).astype(F32), 0.0)
    first =_iota2((c, W), 1) < P
    first_row = _iota2((W, N), 0) < P

    dt = _softplus(dt_ref[...] + dtb_ref[...])
    la = dt * (-jnp.exp(alog_ref[...]))
    laT2 = _softplus(dtT2_ref[...] + dtb2_ref[...]) * (-jnp.exp(alog2_ref[...]))
    if valid < c:
        cmask = jnp.bitwise_and(_iota2((1, L), 1), c - 1) < valid
        la = jnp.where(rmask, la, 0.0)
        laT2 = jnp.where(cmask, laT2, 0.0)
        dt = jnp.where(rmask, dt, 0.0)
    acum = _sel_dot(ltri, la)
    acumT2 = _dot_sel(laT2, utri2)
    alast = acum[c - 1:c, :]
    elast = jnp.exp(alast)

    ch_head = jnp.right_shift(_iota2((H, SSM_D_INNER), 1), int(math.log2(P)))
    expand = (ch_head == _iota2((H, SSM_D_INNER), 0)).astype(F32)
    lane2 = _iota2((H, (H // 2) * L), 1)
    pk_head = 2 * jnp.right_shift(lane2, log2c + 1) + jnp.bitwise_and(jnp.right_shift(lane2, log2c), 1)
    expand2 = (pk_head == _iota2((H, (H // 2) * L), 0)).astype(F32)
    dt_x = _dot_sel(dt, expand)
    ea_x = jnp.exp(_dot_sel(acum, expand))
    ew_x = jnp.exp(_dot_sel(alast - acum, expand))
    dskip_x = _dot_sel(jnp.broadcast_to(dskip_ref[...], (SUBLANES, H)), expand)[0:1, :]
    acum_x2 = _dot_sel(acum, expand2)

    for g0 in range(0, G, SSD_WAVE):
        groups = range(g0, min(g0 + SSD_WAVE, G))
        prs = [g * pairs_per_group + j for g in groups for j in range(pairs_per_group)]
        bg, cg, cb2, xs, xdt, m2, xbd, xw = {}, {}, {}, {}, {}, {}, {}, {}
        for g in groups:
            bg[g] = conv_silu(SSM_D_INNER + g * N)
            cg[g] = conv_silu(SSM_D_INNER + G * N + g * N)
            cb2[g] = _dot_nt(cg[g], jnp.concatenate([bg[g], bg[g]], axis=0))
        for pr in prs:
            sl = slice(pr * W, (pr + 1) * W)
            xs[pr] = conv_silu(pr * W)
            xdt[pr] = xs[pr] * dt_x[:, sl]
            diff = acum_x2[:, pr * L:(pr + 1) * L] - acumT2[pr:pr + 1, :]
            decay = jnp.where(tri2, jnp.exp(jnp.where(tri2, diff, 0.0)), 0.0)
            m2[pr] = (cb2[pr // pairs_per_group] * decay).astype(BF16)
            xbd[pr] = jnp.concatenate([jnp.where(first, xdt[pr], 0.0),
                                       jnp.where(first, 0.0, xdt[pr])], axis=0).astype(BF16)
            xw[pr] = (xdt[pr] * ew_x[:, sl]).astype(BF16)
        y_intra, y_inter, upd = {}, {}, {}
        for pr in prs:
            g = pr // pairs_per_group
            sl = slice(pr * W, (pr + 1) * W)
            y_intra[pr] = _dot(m2[pr], xbd[pr])
            y_inter[pr] = _dot_nt(cg[g], s_ref[sl, :])
            upd[pr] = _dot_tn(xw[pr], bg[g])
        gated, sumsq = {}, {}
        for pr in prs:
            g = pr // pairs_per_group
            sl = slice(pr * W, (pr + 1) * W)
            y = y_intra[pr] + y_inter[pr] * ea_x[:, sl] + dskip_x[:, sl] * xs[pr]
            scale = jnp.where(first_row, elast[:, 2 * pr:2 * pr + 1], elast[:, 2 * pr + 1:2 * pr + 2])
            s_ref[sl, :] = scale * s_ref[sl, :] + upd[pr]
            y = y * _silu(z_ref[:, sl])
            ss = jnp.sum(y * y, axis=-1, keepdims=True)
            sumsq[g] = ss if g not in sumsq else sumsq[g] + ss
            gated[pr] = y
        for pr in prs:
            sl = slice(pr * W, (pr + 1) * W)
            rs = lax.rsqrt(sumsq[pr // pairs_per_group] * (1.0 / (pairs_per_group * W)) + EPS)
            y_ref[:, sl] = ((gated[pr] * rs) * normw_ref[:, sl]).astype(y_ref.dtype)
    _conv_carry(xpad_ref, bufout_ref, c, valid)


def _layer_state_io(kern, n_in, s_all, s_acc, layer, rows, cols):
    n_layers, b = s_all.shape[:2]
    first = s_acc is None
    in_spec = pl.BlockSpec((None, None, rows, cols), lambda bb, ii: (layer, bb, 0, 0))
    out_spec = (pl.BlockSpec((n_layers, None, rows, cols), lambda bb, ii: (0, bb, 0, 0)) if first
                else in_spec)
    shape = jax.ShapeDtypeStruct((n_layers, b, rows, cols), F32)
    s_pos = n_in + 1

    def adapted(*refs):
        refs = list(refs) if first else list(refs[:n_in]) + list(refs[n_in + 1:])
        if first:
            s_full = refs[s_pos]
            refs[s_pos] = s_full.at[layer]

            @pl.when(pl.program_id(1) == 0)
            def _():
                for other in range(n_layers):
                    if other != layer:
                        s_full[other] = jnp.zeros((rows, cols), F32)
        return kern(*refs)

    extra_args = [] if first else [s_acc]
    extra_specs = [] if first else [pl.BlockSpec(memory_space=pl.ANY)]
    aliases = {} if first else {n_in: 1}
    return (adapted, s_all.reshape(n_layers, b, rows, cols), in_spec, out_spec, shape,
            extra_args, extra_specs, aliases)


def _ssd(z, xbc, dt, buf, s_all, s_acc, p, i, *, b, n, c, valid, row0):
    t, blk0 = b * n * c, row0 // c
    H, P, N = SSM_HEADS, SSM_HEAD_DIM, SSM_D_STATE
    C = SSM_CONV_DIM
    tok = lambda w: pl.BlockSpec((c, w), lambda bb, ii: (blk0 + bb * n + ii, 0))
    tok_out = lambda w: pl.BlockSpec((c, w), lambda bb, ii: (bb * n + ii, 0))
    full2 = lambda a: pl.BlockSpec(a.shape, lambda bb, ii: (0, 0))
    dtT2 = _pair_packed_transpose(dt[row0:row0 + t], b, n, c)
    convw = p["ssm_conv_w"][i]
    convb = p["ssm_conv_b"][i].reshape(1, C)
    dtb = p["ssm_dt_bias"][i].reshape(1, H)
    dtb2 = jnp.repeat(p["ssm_dt_bias"][i].reshape(H // 2, 2), c, axis=1)
    alog = p["ssm_A_log"][i].reshape(1, H)
    alog2 = jnp.repeat(p["ssm_A_log"][i].reshape(H // 2, 2), c, axis=1)
    dskip = p["ssm_D"][i].reshape(1, H)
    normw = p["ssm_norm_w"][i].reshape(1, SSM_D_INNER)
    small = [convw, convb, dtb, dtb2, alog, alog2, dskip, normw]
    n_in = 6 + len(small)
    kern, s_in, s_spec, s_out_spec, s_shape, acc_args, acc_specs, aliases = _layer_state_io(
        functools.partial(_ssd_kernel, c=c, valid=valid), n_in, s_all, s_acc, i, H * P, N)
    y, s_new, buf_new = pl.pallas_call(
        kern,
        grid=(b, n),
        in_specs=[tok(SSM_D_INNER), tok(C), tok(H),
                  pl.BlockSpec((None, None, H // 2, 2 * c), lambda bb, ii: (bb, ii, 0, 0)),
                  pl.BlockSpec((None, CONV_W - 1, C), lambda bb, ii: (bb, 0, 0)),
                  s_spec]
                 + [full2(a) for a in small] + acc_specs,
        out_specs=[tok_out(SSM_D_INNER), s_out_spec,
                   pl.BlockSpec((None, CONV_W - 1, C), lambda bb, ii: (bb, 0, 0))],
        out_shape=[jax.ShapeDtypeStruct((t, SSM_D_INNER), BF16), s_shape,
                   jax.ShapeDtypeStruct((b, CONV_W - 1, C), F32)],
        scratch_shapes=[pltpu.VMEM((8 + c, C), F32)],
        input_output_aliases=aliases,
        compiler_params=_params(("arbitrary", "arbitrary")),
        name="ssd_scan",
    )(z, xbc, dt, dtT2, buf, s_in, *small, *acc_args)
    return y, s_new, buf_new


def _gla_kernel(hq_ref, hf_ref, hi_ref, hg_ref, s0_ref, lb_ref, normw_ref,
                o_ref, s_ref, *, c, valid, layer):
    H, K = HG_HEADS, HG_HEAD_DIM

    @pl.when(pl.program_id(1) == 0)
    def _():
        s_ref[...] = s0_ref[...]

    lbp = lb_ref[...]
    mx = lbp[0:1, :]
    for j in range(1, N_AB):
        mx = jnp.maximum(mx, lbp[j:j + 1, :])
    e = [jnp.exp(lbp[j:j + 1, :] - mx) for j in range(N_AB)]
    den = e[0]
    for j in range(1, N_AB):
        den = den + e[j]
    lb = e[0] / den
    for j in range(1, layer + 1):
        lb = lb + e[j] / den
    lb = lb - e[0] / den

    rmask = _iota2((c, 1), 0) < valid
    one_m_lb = 1.0 - lb

    def masked(t):
        return jnp.where(rmask, t, 0.0) if valid < c else t

    logf, ks = [], []
    for h in range(H):
        sl = slice(h * K, (h + 1) * K)
        sig = _sigmoid(hf_ref[:, sl])
        logf.append(masked(jnp.log(jnp.maximum(lb[:, sl] + one_m_lb[:, sl] * sig, TINY))))
        ks.append(masked(one_m_lb[:, sl] * (1.0 - sig)))

    row = _iota2((c, c), 0)
    col = _iota2((c, c), 1)
    ltri = (col <= row).astype(F32)
    g_all = _sel_dot(ltri, jnp.concatenate(logf, axis=1))

    rowv = _iota2((c, 1), 0)
    eye = row == col
    levels = []
    m = c // 2
    while m >= 1:
        sh2m = int(math.log2(2 * m))
        t_up = jnp.bitwise_and(row, 2 * m - 1) - m
        s_lo = (m - 1) - jnp.bitwise_and(col, 2 * m - 1)
        other_blk = -jnp.bitwise_xor(jnp.right_shift(row, sh2m), jnp.right_shift(col, sh2m))
        levels.append((m, jnp.bitwise_and(rowv, 2 * m - 1) >= m,
                       jnp.bitwise_or(jnp.bitwise_or(t_up, s_lo), other_blk) >= 0))
        m //= 2

    for h0 in range(0, H, GLA_WAVE):
        heads = range(h0, min(h0 + GLA_WAVE, H))
        att, qs, vs = {}, {}, {}
        for h in heads:
            sl = slice(h * K, (h + 1) * K)
            g = g_all[:, sl]
            q = masked(_silu(hq_ref[:, sl]))
            k = ks[h]
            a = jnp.where(eye, _dot_nt(q, k), 0.0)
            for m, upper, pair_mask in levels:
                d = g - _block_mid_rows(g, m, rowv)
                zl = (jnp.where(upper, q, k) * jnp.exp(jnp.where(upper, d, -d))).astype(BF16)
                a = a + jnp.where(pair_mask, _dot_nt(zl, zl), 0.0)
            att[h], qs[h], vs[h] = a, q, masked(hi_ref[:, sl]).astype(BF16)
        for h in heads:
            sl = slice(h * K, (h + 1) * K)
            g = g_all[:, sl]
            glast = g[c - 1:c, :]
            sh = s_ref[sl, :]
            y = _dot(att[h], vs[h]) + _dot(qs[h] * jnp.exp(g), sh)
            dec_col = jnp.exp(jnp.broadcast_to(glast, (SUBLANES, K)).T[:, 0:1])
            s_ref[sl, :] = dec_col * sh + _dot_tn(ks[h] * jnp.exp(glast - g), vs[h])
            yn = y * lax.rsqrt(jnp.mean(y * y, axis=-1, keepdims=True) + EPS)
            o_ref[:, sl] = (yn * normw_ref[:, sl] * _silu(hg_ref[:, sl])).astype(o_ref.dtype)


def _gla(hh, s_all, s_acc, p, i, *, b, n, c, valid, row0):
    t, blk0 = b * n * c, row0 // c
    H, K = HG_HEADS, HG_HEAD_DIM
    W = HG_WIDTH
    tok = lambda j: pl.BlockSpec((c, W), lambda bb, ii: (blk0 + bb * n + ii, j))
    full2 = lambda a: pl.BlockSpec(a.shape, lambda bb, ii: (0, 0))
    lb = p["hg_lower_bounds"]
    normw = p["hg_norm_w"][i].reshape(1, W)
    n_in = 7
    kern, s_in, s_spec, s_out_spec, s_shape, acc_args, acc_specs, aliases = _layer_state_io(
        functools.partial(_gla_kernel, c=c, valid=valid, layer=i), n_in, s_all, s_acc, i, H * K, K)
    o, s_new = pl.pallas_call(
        kern,
        grid=(b, n),
        in_specs=[tok(0), tok(1), tok(2), tok(3), s_spec, full2(lb), full2(normw)] + acc_specs,
        out_specs=[pl.BlockSpec((c, W), lambda bb, ii: (bb * n + ii, 0)), s_out_spec],
        out_shape=[jax.ShapeDtypeStruct((t, W), BF16), s_shape],
        input_output_aliases=aliases,
        compiler_params=_params(("arbitrary", "arbitrary")),
        name="gla_scan",
    )(hh, hh, hh, hh, s_in, lb, normw, *acc_args)
    return o, s_new


def _split_bf16(x):
    hi = x.astype(BF16)
    return hi, (x - hi.astype(F32)).astype(BF16)


def _gdn_kernel(qkv_ref, z_ref, ba_ref, aT2_ref, buf_ref, s0_ref, convw_ref,
                dtb_ref, dtb2_ref, alog_ref, alog2_ref, normw_ref,
                o_ref, s_ref, bufout_ref, xpad_ref, *, c, valid, wave):
    HV, HK, K, V = GDN_V_HEADS, GDN_K_HEADS, GDN_HEAD_K, GDN_HEAD_V
    L = 2 * c
    log2c = int(math.log2(c))

    @pl.when(pl.program_id(1) == 0)
    def _():
        s_ref[...] = s0_ref[...]

    _conv_begin(xpad_ref, qkv_ref, buf_ref, c)
    rmask = _iota2((c, 1), 0) < valid

    def conv_silu(col0):
        t = _silu(_conv_tile(xpad_ref, convw_ref, c, col0, K))
        return jnp.where(rmask, t, 0.0) if valid < c else t

    row = _iota2((c, c), 0)
    col = _iota2((c, c), 1)
    ltri = (col <= row).astype(F32)
    rowp = _iota2((c, L), 0)
    lanep = _iota2((c, L), 1)
    left = lanep < c
    sp = jnp.bitwise_and(lanep, c - 1)
    tri2 = sp <= rowp
    strict2 = sp < rowp
    eye2 = (sp == rowp).astype(F32)
    rl = _iota2((L, L), 0)
    cl = _iota2((L, L), 1)
    utri2 = jnp.where(jnp.right_shift(rl, log2c) == jnp.right_shift(cl, log2c),
                      (rl <= cl).astype(F32), 0.0)

    ba = ba_ref[...]
    beta = _sigmoid(ba[:, :HV])
    gl = -jnp.exp(alog_ref[...]) * _softplus(ba[:, HV:] + dtb_ref[...])
    glT2 = -jnp.exp(alog2_ref[...]) * _softplus(aT2_ref[...] + dtb2_ref[...])
    if valid < c:
        cmask = jnp.bitwise_and(_iota2((1, L), 1), c - 1) < valid
        beta = jnp.where(rmask, beta, 0.0)
        gl = jnp.where(rmask, gl, 0.0)
        glT2 = jnp.where(cmask, glT2, 0.0)
    gcum = _sel_dot(ltri, gl)
    gcumT2 = _dot_sel(glT2, utri2)
    egc = jnp.exp(gcum)
    glast = gcum[c - 1:c, :]
    elast = jnp.exp(glast)
    edec = jnp.exp(glast - gcum)

    n_sq = log2c - 1
    normw = normw_ref[...]
    zeros_cv = jnp.zeros((c, V), F32)

    def pair_cols(a, hk):
        return jnp.where(left, a[:, 2 * hk:2 * hk + 1], a[:, 2 * hk + 1:2 * hk + 2])

    def block_diag(x2):
        return jnp.concatenate([jnp.where(left, x2, 0.0), jnp.where(left, 0.0, x2)], axis=0)

    stack_bf16 = c % (2 * SUBLANES) == 0
    left_b16 = left.astype(BF16)
    right_b16 = 1.0 - left_b16

    def split_pair(x2):
        hi, lo = _split_bf16(x2)
        if stack_bf16:
            bd = tuple(jnp.concatenate([t * left_b16, t * right_b16], axis=0) for t in (hi, lo))
        else:
            bd = _split_bf16(block_diag(x2))
        return (hi, lo), bd

    def block_diag_b16(x2):
        if not stack_bf16:
            return block_diag(x2).astype(BF16)
        xb = x2.astype(BF16)
        return jnp.concatenate([xb * left_b16, xb * right_b16], axis=0)

    def pair_product(lhs_split, rhs_bd):
        lh, ll = lhs_split
        rh, rlo = rhs_bd
        return (jnp.dot(lh, rh, preferred_element_type=F32)
                + jnp.dot(ll, rh, preferred_element_type=F32)
                + jnp.dot(lh, rlo, preferred_element_type=F32))

    for w0 in range(0, HK, wave):
        hks = list(range(w0, min(w0 + wave, HK)))
        qn, kn, gam, qkg, pw, tm = {}, {}, {}, {}, {}, {}
        for hk in hks:
            qh = conv_silu(hk * K)
            kh = conv_silu(GDN_KEY_DIM + hk * K)
            qn[hk] = qh * lax.rsqrt(jnp.sum(qh * qh, axis=-1, keepdims=True) + EPS) * (K ** -0.5)
            kn[hk] = kh * lax.rsqrt(jnp.sum(kh * kh, axis=-1, keepdims=True) + EPS)
            gq = _dot_nt(jnp.concatenate([kn[hk], qn[hk]], axis=0),
                         jnp.concatenate([kn[hk], kn[hk]], axis=0))
            diff = pair_cols(gcum, hk) - gcumT2[hk:hk + 1, :]
            gam[hk] = jnp.where(tri2, jnp.exp(jnp.where(tri2, diff, 0.0)), 0.0)
            pw[hk] = -jnp.where(strict2, pair_cols(beta, hk) * gq[:c] * gam[hk], 0.0)
            qkg[hk] = gq[c:] * gam[hk]
            tm[hk] = eye2 + pw[hk]
        pw_split = {hk: split_pair(pw[hk]) for hk in hks}
        for _ in range(n_sq):
            for hk in hks:
                pw_split[hk] = split_pair(pair_product(*pw_split[hk]))
            for hk in hks:
                tm[hk] = tm[hk] + pair_product(_split_bf16(tm[hk]), pw_split[hk][1])
        wu, ws_qs, v_new = {}, {}, {}
        for hk in hks:
            rhs = []
            for j in range(2):
                h = 2 * hk + j
                bh = beta[:, h:h + 1]
                vh = conv_silu(2 * GDN_KEY_DIM + h * V)
                rhs.append(jnp.concatenate([kn[hk] * (bh * egc[:, h:h + 1]), vh * bh], axis=1))
            wu[hk] = _dot(block_diag_b16(tm[hk]), jnp.concatenate(rhs, axis=0))
        for hk in hks:
            for j in range(2):
                h = 2 * hk + j
                wq = jnp.concatenate([wu[hk][j * c:(j + 1) * c, :K], qn[hk] * egc[:, h:h + 1]], axis=0)
                ws_qs[h] = _dot(wq, s_ref[h * K:(h + 1) * K, :])
                v_new[h] = wu[hk][j * c:(j + 1) * c, K:] - ws_qs[h][:c]
        for hk in hks:
            ha, hb = 2 * hk, 2 * hk + 1
            y2 = _dot(block_diag_b16(qkg[hk]), jnp.concatenate([v_new[ha], v_new[hb]], axis=0))
            kdec = jnp.concatenate([kn[hk] * edec[:, ha:ha + 1], kn[hk] * edec[:, hb:hb + 1]], axis=0)
            vbd = jnp.concatenate([jnp.concatenate([v_new[ha], zeros_cv], axis=1),
                                   jnp.concatenate([zeros_cv, v_new[hb]], axis=1)], axis=0)
            upd = _dot_tn(kdec, vbd)
            for j, h in enumerate((ha, hb)):
                sl = slice(h * K, (h + 1) * K)
                s_ref[sl, :] = elast[:, h:h + 1] * s_ref[sl, :] + upd[:, j * V:(j + 1) * V]
                y = ws_qs[h][c:] + y2[j * c:(j + 1) * c]
                yn = (y * lax.rsqrt(jnp.mean(y * y, axis=-1, keepdims=True) + EPS)) * normw
                o_ref[:, h * V:(h + 1) * V] = (yn * _silu(z_ref[:, h * V:(h + 1) * V])).astype(o_ref.dtype)
    _conv_carry(xpad_ref, bufout_ref, c, valid)


def _pair_packed_transpose(a, b, n, c):
    hh = a.shape[-1] // 2
    x = a.reshape(b, n, c, hh, 2)
    return jnp.transpose(x, (0, 1, 3, 4, 2)).reshape(b, n, hh, 2 * c)


def _gdn(qkv, z, ba, buf, s_all, s_acc, p, i, *, b, n, c, valid, row0):
    t, blk0 = b * n * c, row0 // c
    HV, HK, K, V = GDN_V_HEADS, GDN_K_HEADS, GDN_HEAD_K, GDN_HEAD_V
    C = GDN_CONV_DIM
    tok = lambda w: pl.BlockSpec((c, w), lambda bb, ii: (blk0 + bb * n + ii, 0))
    tok_out = lambda w: pl.BlockSpec((c, w), lambda bb, ii: (bb * n + ii, 0))
    full2 = lambda a: pl.BlockSpec(a.shape, lambda bb, ii: (0, 0))
    aT2 = _pair_packed_transpose(ba[row0:row0 + t, HV:], b, n, c)
    convw = p["gdn_conv_w"][i]
    dtb = p["gdn_dt_bias"][i].reshape(1, HV)
    dtb2 = jnp.repeat(p["gdn_dt_bias"][i].reshape(HK, 2), c, axis=1)
    alog = p["gdn_A_log"][i].reshape(1, HV)
    alog2 = jnp.repeat(p["gdn_A_log"][i].reshape(HK, 2), c, axis=1)
    normw = p["gdn_norm_w"][i].reshape(1, V)
    small = [convw, dtb, dtb2, alog, alog2, normw]
    n_in = 6 + len(small)
    wave = GDN_WAVE if c >= CHUNK else GDN_WAVE_SHORT_CHUNK
    kern, s_in, s_spec, s_out_spec, s_shape, acc_args, acc_specs, aliases = _layer_state_io(
        functools.partial(_gdn_kernel, c=c, valid=valid, wave=wave), n_in, s_all, s_acc, i, HV * K, V)
    o, s_new, buf_new = pl.pallas_call(
        kern,
        grid=(b, n),
        in_specs=[tok(C), tok(GDN_VAL_DIM), tok(2 * HV),
                  pl.BlockSpec((None, None, HK, 2 * c), lambda bb, ii: (bb, ii, 0, 0)),
                  pl.BlockSpec((None, CONV_W - 1, C), lambda bb, ii: (bb, 0, 0)),
                  s_spec]
                 + [full2(a) for a in small] + acc_specs,
        out_specs=[tok_out(GDN_VAL_DIM), s_out_spec,
                   pl.BlockSpec((None, CONV_W - 1, C), lambda bb, ii: (bb, 0, 0))],
        out_shape=[jax.ShapeDtypeStruct((t, GDN_VAL_DIM), BF16), s_shape,
                   jax.ShapeDtypeStruct((b, CONV_W - 1, C), F32)],
        scratch_shapes=[pltpu.VMEM((8 + c, C), F32)],
        input_output_aliases=aliases,
        compiler_params=_params(("arbitrary", "arbitrary")),
        name="gdn_scan",
    )(qkv, z, ba, aT2, buf, s_in, *small, *acc_args)
    return o, s_new, buf_new


def _trunk(groups, p):
    d = groups[0]["x"].shape[-1]
    row0 = 0
    for gr in groups:
        b, lp, _ = gr["x"].shape
        gr["dims"] = dict(b=b, n=lp // gr["c"], c=gr["c"], valid=gr["valid"], row0=row0)
        gr["rows"] = b * lp
        gr["new"] = dict(ssm=None, hg=None, gdn=None, ssm_conv=[], gdn_conv=[])
        row0 += b * lp
    xt = jnp.concatenate([gr["x"].reshape(gr["rows"], d) for gr in groups], axis=0)
    for l in range(DEPTH):
        i = l // 2
        h = _rmsnorm(xt, p["norm_mix"][l], BF16)
        if l % 2 == 0:
            in_proj = functools.partial(_matmul, h, p["w_in_ab_t"], i, w_rows_are_outputs=True)
            z = in_proj(0, SSM_D_INNER)
            xbc = in_proj(AB_OFF_XBC, SSM_CONV_DIM)
            dt = in_proj(AB_OFF_DT, SSM_HEADS)
            hh = in_proj(AB_OFF_HG, 4 * HG_WIDTH)
            ys, os = [], []
            for gr in groups:
                new = gr["new"]
                y, new["ssm"], b1 = _ssd(z, xbc, dt, gr["ssm_conv"][i], gr["ssm"], new["ssm"],
                                         p, i, **gr["dims"])
                o, new["hg"] = _gla(hh, gr["hg"], new["hg"], p, i, **gr["dims"])
                new["ssm_conv"].append(b1)
                ys.append(y)
                os.append(o)
            xt = _matmul((ys, os), p["w_out_ab"], i, 0, D_MODEL, epilogue="residual", residual=xt)
        else:
            in_proj = functools.partial(_matmul, h, p["w_in_c_t"], i, w_rows_are_outputs=True)
            qkv = in_proj(0, GDN_CONV_DIM)
            z = in_proj(C_OFF_Z, GDN_VAL_DIM)
            ba = in_proj(C_OFF_BA, 2 * GDN_V_HEADS)
            os = []
            for gr in groups:
                new = gr["new"]
                o, new["gdn"], b3 = _gdn(qkv, z, ba, gr["gdn_conv"][i], gr["gdn"], new["gdn"],
                                         p, i, **gr["dims"])
                new["gdn_conv"].append(b3)
                os.append(o)
            xt = _matmul((os,), p["w_out_c"], i, 0, D_MODEL, epilogue="residual", residual=xt)
        h = _rmsnorm(xt, p["norm_mlp"][l], BF16)
        u = _matmul(h, p["w_ff1"], l, 0, FFN_HIDDEN, epilogue="relu2", out_dtype=BF16)
        xt = _matmul(u, p["w_ff2"], l, 0, D_MODEL, epilogue="residual", residual=xt)
    outs = []
    for gr in groups:
        new = gr["new"]
        y = _rmsnorm(xt, p["norm_final"], F32, gr["dims"]["row0"], gr["rows"]).reshape(gr["x"].shape)
        outs.append((y, new["ssm"].reshape(gr["ssm"].shape), jnp.stack(new["ssm_conv"]),
                     new["hg"].reshape(gr["hg"].shape), new["gdn"].reshape(gr["gdn"].shape),
                     jnp.stack(new["gdn_conv"])))
    return outs


def kernel(x_prompt, x_sample, state_ssm, state_ssm_conv, state_hgrn, state_gdn, state_gdn_conv, norm_mix, norm_mlp, norm_final, w_in_ab, ssm_conv_w, ssm_conv_b, ssm_dt_bias, ssm_A_log, ssm_D, ssm_norm_w, hg_lower_bounds, hg_norm_w, w_out_ab, w_in_c, gdn_conv_w, gdn_dt_bias, gdn_A_log, gdn_norm_w, w_out_c, w_ff1, w_ff2):
    p = dict(norm_mix=norm_mix, norm_mlp=norm_mlp, norm_final=norm_final, w_in_ab=w_in_ab,
             ssm_conv_w=ssm_conv_w, ssm_conv_b=ssm_conv_b, ssm_dt_bias=ssm_dt_bias, ssm_A_log=ssm_A_log,
             ssm_D=ssm_D, ssm_norm_w=ssm_norm_w, hg_lower_bounds=hg_lower_bounds, hg_norm_w=hg_norm_w,
             w_out_ab=w_out_ab, w_in_c=w_in_c, gdn_conv_w=gdn_conv_w, gdn_dt_bias=gdn_dt_bias,
             gdn_A_log=gdn_A_log, gdn_norm_w=gdn_norm_w, w_out_c=w_out_c, w_ff1=w_ff1, w_ff2=w_ff2)
    p["w_in_ab_t"] = jnp.swapaxes(w_in_ab, 1, 2)
    p["w_in_c_t"] = jnp.swapaxes(w_in_c, 1, 2)

    bp, seq, _ = x_prompt.shape
    z_ssm = jnp.zeros((N_AB, bp, SSM_HEADS, SSM_HEAD_DIM, SSM_D_STATE), F32)
    z_ssm_conv = jnp.zeros((N_AB, bp, CONV_W - 1, SSM_CONV_DIM), F32)
    z_hg = jnp.zeros((N_AB, bp, HG_HEADS, HG_HEAD_DIM, HG_HEAD_DIM), F32)
    z_gdn = jnp.zeros((N_C, bp, GDN_V_HEADS, GDN_HEAD_K, GDN_HEAD_V), F32)
    z_gdn_conv = jnp.zeros((N_C, bp, CONV_W - 1, GDN_CONV_DIM), F32)
    c_p = min(CHUNK, seq)
    prompt = dict(x=x_prompt, c=c_p, valid=c_p, ssm=z_ssm, ssm_conv=z_ssm_conv, hg=z_hg, gdn=z_gdn,
                  gdn_conv=z_gdn_conv)

    bs, dec, _ = x_sample.shape
    c_s = -(-dec // SUBLANES) * SUBLANES
    xs_pad = jnp.pad(x_sample, ((0, 0), (0, c_s - dec), (0, 0)))
    sample = dict(x=xs_pad, c=c_s, valid=dec, ssm=state_ssm, ssm_conv=state_ssm_conv, hg=state_hgrn,
                  gdn=state_gdn, gdn_conv=state_gdn_conv)
    outs_p, outs_s = _trunk([prompt, sample], p)
    y_sample = outs_s[0][:, :dec]
    return (outs_p[0], y_sample) + tuple(outs_p[1:]) + tuple(outs_s[1:])
```

```python
import functools
import math

import jax
import jax.numpy as jnp
from jax import lax
from jax.experimental import pallas as pl
from jax.experimental.pallas import tpu as pltpu

F32 = jnp.float32
BF16 = jnp.bfloat16

D_MODEL = 2048
DEPTH = 4
N_AB = (DEPTH + 1) // 2
N_C = DEPTH // 2
CONV_W = 4
CHUNK = 64
EPS = 1e-6
TINY = 1e-30
LOG2_E = math.log2(math.e)

SSM_D_INNER = D_MODEL
SSM_HEAD_DIM = 64
SSM_HEADS = SSM_D_INNER // SSM_HEAD_DIM
SSM_GROUPS = 4
SSM_D_STATE = 128
SSM_CONV_DIM = SSM_D_INNER + 2 * SSM_GROUPS * SSM_D_STATE

HG_WIDTH = D_MODEL
HG_HEAD_DIM = 128
HG_HEADS = HG_WIDTH // HG_HEAD_DIM

GDN_HEAD_K = 128
GDN_HEAD_V = 128
GDN_K_HEADS = D_MODEL // GDN_HEAD_K
GDN_V_HEADS = 2 * GDN_K_HEADS
GDN_KEY_DIM = GDN_K_HEADS * GDN_HEAD_K
GDN_VAL_DIM = GDN_V_HEADS * GDN_HEAD_V
GDN_CONV_DIM = 2 * GDN_KEY_DIM + GDN_VAL_DIM

FFN_HIDDEN = 4 * D_MODEL

AB_OFF_XBC = SSM_D_INNER
AB_OFF_DT = AB_OFF_XBC + SSM_CONV_DIM
AB_OFF_HG = AB_OFF_DT + SSM_HEADS
C_OFF_Z = GDN_CONV_DIM
C_OFF_BA = C_OFF_Z + GDN_VAL_DIM

SUBLANES = 8
VMEM_LIMIT = 52 * 1024 * 1024
GDN_WAVE = 8
GDN_WAVE_SHORT_CHUNK = 16
GLA_WAVE = 16
SSD_WAVE = 4


def _params(sem):
    return pltpu.CompilerParams(dimension_semantics=sem, vmem_limit_bytes=VMEM_LIMIT)


def _dot(a, b):
    return jnp.dot(a.astype(BF16), b.astype(BF16), preferred_element_type=F32)


def _dot_nt(a, b):
    return lax.dot_general(a.astype(BF16), b.astype(BF16), (((1,), (1,)), ((), ())),
                           preferred_element_type=F32)


def _dot_tn(a, b):
    return lax.dot_general(a.astype(BF16), b.astype(BF16), (((0,), (0,)), ((), ())),
                           preferred_element_type=F32)


def _split3_bf16(x):
    p0 = x.astype(BF16)
    r1 = x - p0.astype(F32)
    p1 = r1.astype(BF16)
    return p0, p1, (r1 - p1.astype(F32)).astype(BF16)


def _sel_dot(sel, x):
    s = sel.astype(BF16)
    p0, p1, p2 = _split3_bf16(x)
    return (jnp.dot(s, p0, preferred_element_type=F32) + jnp.dot(s, p1, preferred_element_type=F32)
            + jnp.dot(s, p2, preferred_element_type=F32))


def _dot_sel(x, sel):
    s = sel.astype(BF16)
    p0, p1, p2 = _split3_bf16(x)
    return (jnp.dot(p0, s, preferred_element_type=F32) + jnp.dot(p1, s, preferred_element_type=F32)
            + jnp.dot(p2, s, preferred_element_type=F32))


def _sigmoid(x):
    return 1.0 / (1.0 + jnp.exp2(x * (-LOG2_E)))


def _silu(x):
    return x * _sigmoid(x)


def _softplus(x):
    return jnp.maximum(x, 0.0) + jnp.log1p(jnp.exp(-jnp.abs(x)))


def _iota2(shape, dim):
    return lax.broadcasted_iota(jnp.int32, shape, dim)


def _block_mid_rows(g, m, rowv):
    c, w = g.shape
    if 2 * m >= SUBLANES:
        return jnp.concatenate([jnp.broadcast_to(g[b0 + m:b0 + m + 1, :], (2 * m, w))
                                for b0 in range(0, c, 2 * m)], axis=0)
    pos = jnp.bitwise_and(rowv, 2 * m - 1)
    out = g
    for off in range(1 - m, m + 1):
        if off != 0:
            out = jnp.where(pos == m - off, pltpu.roll(g, (-off) % c, 0), out)
    return out


def _conv_begin(xpad_ref, x_ref, buf_ref, c):
    @pl.when(pl.program_id(1) == 0)
    def _():
        xpad_ref[5:8, :] = buf_ref[...]

    xpad_ref[8:8 + c, :] = x_ref[...]


def _conv_tile(xpad_ref, w_ref, c, col0, width):
    sl = slice(col0, col0 + width)
    y = xpad_ref[5:5 + c, sl] * w_ref[0:1, sl]
    for j in range(1, CONV_W):
        y = y + xpad_ref[5 + j:5 + j + c, sl] * w_ref[j:j + 1, sl]
    return y


def _conv_carry(xpad_ref, bufout_ref, c, valid_last):
    @pl.when(pl.program_id(1) == pl.num_programs(1) - 1)
    def _():
        bufout_ref[...] = xpad_ref[8 + valid_last - 3:8 + valid_last, :]

    xpad_ref[5:8, :] = xpad_ref[8 + c - 3:8 + c, :]


def _rmsnorm_kernel(x_ref, w_ref, o_ref):
    x = x_ref[...]
    ms = jnp.mean(x * x, axis=-1, keepdims=True)
    o_ref[...] = ((x * lax.rsqrt(ms + EPS)) * w_ref[...]).astype(o_ref.dtype)


def _rmsnorm(x, w, out_dtype, row0=0, rows=None):
    d = x.shape[1]
    t = x.shape[0] if rows is None else rows
    tm = min(512, t)
    blk0 = row0 // tm
    return pl.pallas_call(
        _rmsnorm_kernel,
        grid=(t // tm,),
        in_specs=[pl.BlockSpec((tm, d), lambda i: (blk0 + i, 0)),
                  pl.BlockSpec((1, d), lambda i: (0, 0))],
        out_specs=pl.BlockSpec((tm, d), lambda i: (i, 0)),
        out_shape=jax.ShapeDtypeStruct((t, d), out_dtype),
        compiler_params=_params(("arbitrary",)),
        name="rmsnorm",
    )(x, w.reshape(1, d))


def _rmsnorm_proj_kernel(x_ref, w_ref, wp_ref, o_ref, p_ref):
    x = x_ref[...]
    ms = jnp.mean(x * x, axis=-1, keepdims=True)
    h = ((x * lax.rsqrt(ms + EPS)) * w_ref[...]).astype(o_ref.dtype)
    o_ref[...] = h
    p_ref[...] = lax.dot_general(h, wp_ref[...].astype(BF16), (((1,), (1,)), ((), ())),
                                 preferred_element_type=F32)


def _rmsnorm_proj(x, w, w_t, layer, col0, n):
    t, d = x.shape
    tm = min(512, t)
    n_total = w_t.shape[1]
    row0 = layer * n_total + col0
    return pl.pallas_call(
        _rmsnorm_proj_kernel,
        grid=(t // tm,),
        in_specs=[pl.BlockSpec((tm, d), lambda i: (i, 0)),
                  pl.BlockSpec((1, d), lambda i: (0, 0)),
                  pl.BlockSpec((pl.Element(n), pl.Element(d)), lambda i: (row0, 0))],
        out_specs=[pl.BlockSpec((tm, d), lambda i: (i, 0)),
                   pl.BlockSpec((tm, n), lambda i: (i, 0))],
        out_shape=[jax.ShapeDtypeStruct((t, d), BF16), jax.ShapeDtypeStruct((t, n), F32)],
        compiler_params=_params(("arbitrary",)),
        name="rmsnorm_proj",
    )(x, w.reshape(1, d), w_t.reshape(w_t.shape[0] * n_total, d))


def _mm_kernel(*refs, n_seg, group_tiles, epilogue, w_rows_are_outputs):
    n_a = n_seg * len(group_tiles)
    a_refs, refs = refs[:n_a], refs[n_a:]
    if epilogue == "residual":
        w_ref, r_ref, o_ref, wb_ref = refs
    else:
        w_ref, o_ref, wb_ref = refs
    i = pl.program_id(1)

    @pl.when(i == 0)
    def _():
        wb_ref[...] = w_ref[...].astype(BF16)

    def row_tile(seg_refs):
        if w_rows_are_outputs:
            acc = lax.dot_general(seg_refs[0][...], wb_ref[...], (((1,), (1,)), ((), ())),
                                  preferred_element_type=F32)
        else:
            acc, k0 = None, 0
            for a_ref in seg_refs:
                k1 = k0 + a_ref.shape[1]
                part = jnp.dot(a_ref[...], wb_ref[k0:k1, :], preferred_element_type=F32)
                acc, k0 = (part if acc is None else acc + part), k1
        if epilogue == "relu2":
            acc = jnp.square(jnp.maximum(acc, 0.0))
        elif epilogue == "residual":
            acc = r_ref[...] + acc
        o_ref[...] = acc.astype(o_ref.dtype)

    tile0 = 0
    for g, n_tiles in enumerate(group_tiles):
        seg_refs = a_refs[g * n_seg:(g + 1) * n_seg]
        if len(group_tiles) == 1:
            row_tile(seg_refs)
        else:
            pl.when(jnp.logical_and(i >= tile0, i < tile0 + n_tiles))(
                functools.partial(row_tile, seg_refs))
        tile0 += n_tiles


def _mm_tiles(t, k, n):
    tn ={2048: 1024, 4096: 512, 8192: 512}[k]
    tm = {2048: 1024, 4096: 1024, 8192: 512}[k]
    return tm, min(tn, n), (2 if k == 2048 else 1)


def _matmul(a, w3, layer, col0, n, *, epilogue="none", residual=None, out_dtype=F32,
            w_rows_are_outputs=False):
    a_segs = a if isinstance(a, (tuple, list)) else (a,)
    a_segs = [seg if isinstance(seg, (tuple, list)) else (seg,) for seg in a_segs]
    group_rows = [arr.shape[0] for arr in a_segs[0]]
    t, k = sum(group_rows), sum(seg[0].shape[1] for seg in a_segs)
    tm, tn, w_bufs = _mm_tiles(t, k, n)
    while any(r % tm for r in group_rows):
        tm //= 2
    group_tiles = tuple(r // tm for r in group_rows)
    tile0 = [sum(group_tiles[:g]) for g in range(len(group_tiles))]

    def a_spec(g, width):
        return pl.BlockSpec((tm, width),
                            lambda j, i: (jnp.clip(i - tile0[g], 0, group_tiles[g] - 1), 0))
    if w_rows_are_outputs:
        n_total = w3.shape[1]
        row0 = layer * n_total + col0
        w3 = w3.reshape(w3.shape[0] * n_total, k)
        w_spec = pl.BlockSpec((pl.Element(tn), pl.Element(k)), lambda j, i: (pl.multiple_of(row0 + j * tn, SUBLANES), 0),
                              pipeline_mode=pl.Buffered(w_bufs))
        wb_shape = (tn, k)
    else:
        cb0 = col0 // tn
        w_spec = pl.BlockSpec((None, k, tn), lambda j, i: (layer, 0, cb0 + j),
                              pipeline_mode=pl.Buffered(w_bufs))
        wb_shape = (k, tn)
    in_specs, args = [], []
    for g in range(len(group_tiles)):
        for seg in a_segs:
            in_specs.append(a_spec(g, seg[g].shape[1]))
            args.append(seg[g])
    in_specs.append(w_spec)
    args.append(w3)
    if epilogue == "residual":
        in_specs.append(pl.BlockSpec((tm, tn), lambda j, i: (i, j)))
        args.append(residual)
    return pl.pallas_call(
        functools.partial(_mm_kernel, n_seg=len(a_segs), group_tiles=group_tiles, epilogue=epilogue,
                          w_rows_are_outputs=w_rows_are_outputs),
        grid=(n // tn, t // tm),
        in_specs=in_specs,
        out_specs=pl.BlockSpec((tm, tn), lambda j, i: (i, j)),
        out_shape=jax.ShapeDtypeStruct((t, n), out_dtype),
        scratch_shapes=[pltpu.VMEM(wb_shape, BF16)],
        compiler_params=_params(("arbitrary", "arbitrary")),
        name="matmul_" + epilogue,
    )(*args)


def _ssd_kernel(z_ref, xbc_ref, dt_ref, dtT2_ref, buf_ref, s0_ref, convw_ref, convb_ref,
                dtb_ref, dtb2_ref, alog_ref, alog2_ref, dskip_ref, normw_ref,
                y_ref, s_ref, bufout_ref, xpad_ref, *, c, valid):
    H, P, N, G = SSM_HEADS, SSM_HEAD_DIM, SSM_D_STATE, SSM_GROUPS
    W = 2 * P
    pairs_per_group = H // G // 2
    L = 2 * c
    log2c = int(math.log2(c))

    @pl.when(pl.program_id(1) == 0)
    def _():
        s_ref[...] = s0_ref[...]

    _conv_begin(xpad_ref, xbc_ref, buf_ref, c)
    rmask = _iota2((c, 1), 0) < valid

    def conv_silu(col0):
        t = _silu(_conv_tile(xpad_ref, convw_ref, c, col0, W) + convb_ref[:, col0:col0 + W])
        return jnp.where(rmask, t, 0.0) if valid < c else t

    row = _iota2((c, c), 0)
    col = _iota2((c, c), 1)
    ltri = (col <= row).astype(F32)
    rowp = _iota2((c, L), 0)
    lanep = _iota2((c, L), 1)
    tri2 = jnp.bitwise_and(lanep, c - 1) <= rowp
    rl = _iota2((L, L), 0)
    cl = _iota2((L, L), 1)
    utri2 = jnp.where(jnp.right_shift(rl, log2c) == jnp.right_shift(cl, log2c),
                      (rl <= cl).astype(F32), 0.0)
    first = _iota2((c, W), 1) < P
    first_row = _iota2((W, N), 0) < P

    dt = _softplus(dt_ref[...] + dtb_ref[...])
    la = dt * (-LOG2_E * jnp.exp(alog_ref[...]))
    laT2 = _softplus(dtT2_ref[...] + dtb2_ref[...]) * (-LOG2_E * jnp.exp(alog2_ref[...]))
    if valid < c:
        cmask = jnp.bitwise_and(_iota2((1, L), 1), c - 1) < valid
        la = jnp.where(rmask, la, 0.0)
        laT2 = jnp.where(cmask, laT2, 0.0)
        dt = jnp.where(rmask, dt, 0.0)
    acum = _sel_dot(ltri, la)
    acumT2 = _dot_sel(laT2, utri2)
    alast = acum[c - 1:c, :]
    elast = jnp.exp2(alast)

    ch_head = jnp.right_shift(_iota2((H, SSM_D_INNER), 1), int(math.log2(P)))
    expand = (ch_head == _iota2((H, SSM_D_INNER), 0)).astype(F32)
    lane2 = _iota2((H, (H // 2) * L), 1)
    pk_head = 2 * jnp.right_shift(lane2, log2c + 1) + jnp.bitwise_and(jnp.right_shift(lane2, log2c), 1)
    expand2 = (pk_head == _iota2((H, (H // 2) * L), 0)).astype(F32)
    dt_x = _dot_sel(dt, expand)
    ea_x = jnp.exp2(_dot_sel(acum, expand))
    ew_x = jnp.exp2(_dot_sel(alast - acum, expand))
    dskip_x = _dot_sel(jnp.broadcast_to(dskip_ref[...], (SUBLANES, H)), expand)[0:1, :]
    acum_x2 = _dot_sel(acum, expand2)

    for g0 in range(0, G, SSD_WAVE):
        groups = range(g0, min(g0 + SSD_WAVE, G))
        prs = [g * pairs_per_group + j for g in groups for j in range(pairs_per_group)]
        bg, cg, cb2, xs, xdt, m2, xbd, xw = {}, {}, {}, {}, {}, {}, {}, {}
        for g in groups:
            bg[g] = conv_silu(SSM_D_INNER + g * N)
            cg[g] = conv_silu(SSM_D_INNER + G * N + g * N)
            cb2[g] = _dot_nt(cg[g], jnp.concatenate([bg[g], bg[g]], axis=0))
        for pr in prs:
            sl = slice(pr * W, (pr + 1) * W)
            xs[pr] = conv_silu(pr * W)
            xdt[pr] = xs[pr] * dt_x[:, sl]
            diff = acum_x2[:, pr * L:(pr + 1) * L] - acumT2[pr:pr + 1, :]
            decay = jnp.where(tri2, jnp.exp2(jnp.where(tri2, diff, 0.0)), 0.0)
            m2[pr] = (cb2[pr // pairs_per_group] * decay).astype(BF16)
            xbd[pr] = jnp.concatenate([jnp.where(first, xdt[pr], 0.0),
                                       jnp.where(first, 0.0, xdt[pr])], axis=0).astype(BF16)
            xw[pr] = (xdt[pr] * ew_x[:, sl]).astype(BF16)
        y_intra, y_inter, upd = {}, {}, {}
        for pr in prs:
            g = pr // pairs_per_group
            sl = slice(pr * W, (pr + 1) * W)
            y_intra[pr] = _dot(m2[pr], xbd[pr])
            y_inter[pr] = _dot_nt(cg[g], s_ref[sl, :])
            upd[pr] = _dot_tn(xw[pr], bg[g])
        gated, sumsq = {}, {}
        for pr in prs:
            g = pr // pairs_per_group
            sl = slice(pr * W, (pr + 1) * W)
            y = y_intra[pr] + y_inter[pr] * ea_x[:, sl] + dskip_x[:, sl] * xs[pr]
            scale = jnp.where(first_row, elast[:, 2 * pr:2 * pr + 1], elast[:, 2 * pr + 1:2 * pr + 2])
            s_ref[sl, :] = scale * s_ref[sl, :] + upd[pr]
            y = y * _silu(z_ref[:, sl])
            ss = jnp.sum(y * y, axis=-1, keepdims=True)
            sumsq[g] = ss if g not in sumsq else sumsq[g] + ss
            gated[pr] = y
        for pr in prs:
            sl = slice(pr * W, (pr + 1) * W)
            rs = lax.rsqrt(sumsq[pr // pairs_per_group] * (1.0 / (pairs_per_group * W)) + EPS)
            y_ref[:, sl] = ((gated[pr] * rs) * normw_ref[:, sl]).astype(y_ref.dtype)
    _conv_carry(xpad_ref, bufout_ref, c, valid)


def _layer_state_io(kern, n_in, s_all, s_acc, layer, rows, cols):
    n_layers, b = s_all.shape[:2]
    first = s_acc is None
    in_spec = pl.BlockSpec((None, None, rows, cols), lambda bb, ii: (layer, bb, 0, 0))
    out_spec = (pl.BlockSpec((n_layers, None, rows, cols), lambda bb, ii: (0, bb, 0, 0)) if first
                else in_spec)
    shape = jax.ShapeDtypeStruct((n_layers, b, rows, cols), F32)
    s_pos = n_in + 1

    def adapted(*refs):
        refs = list(refs) if first else list(refs[:n_in]) + list(refs[n_in + 1:])
        if first:
            s_full = refs[s_pos]
            refs[s_pos] = s_full.at[layer]

            @pl.when(pl.program_id(1) == 0)
            def _():
                for other in range(n_layers):
                    if other != layer:
                        s_full[other] = jnp.zeros((rows, cols), F32)
        return kern(*refs)

    extra_args = [] if first else [s_acc]
    extra_specs = [] if first else [pl.BlockSpec(memory_space=pl.ANY)]
    aliases = {} if first else {n_in: 1}
    return (adapted, s_all.reshape(n_layers, b, rows, cols), in_spec, out_spec, shape,
            extra_args, extra_specs, aliases)


def _ssd(z, xbc, dt, buf, s_all, s_acc, p, i, *, b, n, c, valid, row0):
    t, blk0 = b * n * c, row0 // c
    H, P, N = SSM_HEADS, SSM_HEAD_DIM, SSM_D_STATE
    C = SSM_CONV_DIM
    tok = lambda w: pl.BlockSpec((c, w), lambda bb, ii: (blk0 + bb * n + ii, 0))
    tok_out = lambda w: pl.BlockSpec((c, w), lambda bb, ii: (bb * n + ii, 0))
    full2 = lambda a: pl.BlockSpec(a.shape, lambda bb, ii: (0, 0))
    dtT2 = _pair_packed_transpose(dt[row0:row0 + t], b, n, c)
    convw = p["ssm_conv_w"][i]
    convb = p["ssm_conv_b"][i].reshape(1, C)
    dtb = p["ssm_dt_bias"][i].reshape(1, H)
    dtb2 = jnp.repeat(p["ssm_dt_bias"][i].reshape(H // 2, 2), c, axis=1)
    alog = p["ssm_A_log"][i].reshape(1, H)
    alog2 = jnp.repeat(p["ssm_A_log"][i].reshape(H // 2, 2), c, axis=1)
    dskip = p["ssm_D"][i].reshape(1, H)
    normw = p["ssm_norm_w"][i].reshape(1, SSM_D_INNER)
    small = [convw, convb, dtb, dtb2, alog, alog2, dskip, normw]
    n_in = 6 + len(small)
    kern, s_in, s_spec, s_out_spec, s_shape, acc_args, acc_specs, aliases = _layer_state_io(
        functools.partial(_ssd_kernel, c=c, valid=valid), n_in, s_all, s_acc, i, H * P, N)
    y, s_new, buf_new = pl.pallas_call(
        kern,
        grid=(b, n),
        in_specs=[tok(SSM_D_INNER), tok(C), tok(H),
                  pl.BlockSpec((None, None, H // 2, 2 * c), lambda bb, ii: (bb, ii, 0, 0)),
                  pl.BlockSpec((None, CONV_W - 1, C), lambda bb, ii: (bb, 0, 0)),
                  s_spec]
                 + [full2(a) for a in small] + acc_specs,
        out_specs=[tok_out(SSM_D_INNER), s_out_spec,
                   pl.BlockSpec((None, CONV_W - 1, C), lambda bb, ii: (bb, 0, 0))],
        out_shape=[jax.ShapeDtypeStruct((t, SSM_D_INNER), BF16), s_shape,
                   jax.ShapeDtypeStruct((b, CONV_W - 1, C), F32)],
        scratch_shapes=[pltpu.VMEM((8 + c, C), F32)],
        input_output_aliases=aliases,
        compiler_params=_params(("arbitrary", "arbitrary")),
        name="ssd_scan",
    )(z, xbc, dt, dtT2, buf, s_in, *small, *acc_args)
    return y, s_new, buf_new


def _gla_kernel(hq_ref, hf_ref, hi_ref, hg_ref, s0_ref, lb_ref, normw_ref,
                o_ref, s_ref, *, c, valid, layer):
    H, K = HG_HEADS, HG_HEAD_DIM

    @pl.when(pl.program_id(1) == 0)
    def _():
        s_ref[...] = s0_ref[...]

    lbp = lb_ref[...]
    mx = lbp[0:1, :]
    for j in range(1, N_AB):
        mx = jnp.maximum(mx, lbp[j:j + 1, :])
    e = [jnp.exp(lbp[j:j + 1, :] - mx) for j in range(N_AB)]
    den = e[0]
    for j in range(1, N_AB):
        den = den + e[j]
    lb = e[0] / den
    for j in range(1, layer + 1):
        lb = lb + e[j] / den
    lb = lb - e[0] / den

    rmask = _iota2((c, 1), 0) < valid
    one_m_lb = 1.0 - lb

    def masked(t):
        return jnp.where(rmask, t, 0.0) if valid < c else t

    logf, ks = [], []
    for h in range(H):
        sl = slice(h * K, (h + 1) * K)
        sig = _sigmoid(hf_ref[:, sl])
        logf.append(masked(jnp.log(jnp.maximum(lb[:, sl] + one_m_lb[:, sl] * sig, TINY))))
        ks.append(masked(one_m_lb[:, sl] * (1.0 - sig)))

    row = _iota2((c, c), 0)
    col = _iota2((c, c), 1)
    ltri = (col <= row).astype(F32)
    g_all = _sel_dot(ltri, jnp.concatenate(logf, axis=1)) * LOG2_E

    rowv = _iota2((c, 1), 0)
    eye = row == col
    levels = []
    m = c // 2
    while m >= 1:
        sh2m = int(math.log2(2 * m))
        t_up = jnp.bitwise_and(row, 2 * m - 1) - m
        s_lo = (m - 1) - jnp.bitwise_and(col, 2 * m - 1)
        other_blk = -jnp.bitwise_xor(jnp.right_shift(row, sh2m), jnp.right_shift(col, sh2m))
        upper = jnp.bitwise_and(rowv, 2 * m - 1) >= m
        levels.append((m, upper, jnp.where(upper, 1.0, -1.0),
                       jnp.bitwise_or(jnp.bitwise_or(t_up, s_lo), other_blk) >= 0))
        m //= 2

    for h0 in range(0, H, GLA_WAVE):
        heads = range(h0, min(h0 + GLA_WAVE, H))
        att, qs, vs = {}, {}, {}
        for h in heads:
            sl = slice(h * K, (h + 1) * K)
            g = g_all[:, sl]
            q = masked(_silu(hq_ref[:, sl]))
            k = ks[h]
            a = jnp.where(eye, _dot_nt(q, k), 0.0)
            for m, upper, sign, pair_mask in levels:
                d = (g - _block_mid_rows(g, m, rowv)) * sign
                zl = (jnp.where(upper, q, k) * jnp.exp2(d)).astype(BF16)
                a = a + jnp.where(pair_mask, _dot_nt(zl, zl), 0.0)
            att[h], qs[h], vs[h] = a, q, masked(hi_ref[:, sl]).astype(BF16)
        for h in heads:
            sl = slice(h * K, (h + 1) * K)
            g = g_all[:, sl]
            glast = g[c - 1:c, :]
            sh = s_ref[sl, :]
            y = _dot(att[h], vs[h]) + _dot(qs[h] * jnp.exp2(g), sh)
            dec_col = jnp.exp2(jnp.broadcast_to(glast, (SUBLANES, K)).T[:, 0:1])
            s_ref[sl, :] = dec_col * sh + _dot_tn(ks[h] * jnp.exp2(glast - g), vs[h])
            yn = y * lax.rsqrt(jnp.mean(y * y, axis=-1, keepdims=True) + EPS)
            o_ref[:, sl] = (yn * normw_ref[:, sl] * _silu(hg_ref[:, sl])).astype(o_ref.dtype)


def _gla(hh, s_all, s_acc, p, i, *, b, n, c, valid, row0):
    t, blk0 = b * n * c, row0 // c
    H, K = HG_HEADS, HG_HEAD_DIM
    W = HG_WIDTH
    tok = lambda j: pl.BlockSpec((c, W), lambda bb, ii: (blk0 + bb * n + ii, j))
    full2 = lambda a: pl.BlockSpec(a.shape, lambda bb, ii: (0, 0))
    lb = p["hg_lower_bounds"]
    normw = p["hg_norm_w"][i].reshape(1, W)
    n_in = 7
    kern, s_in, s_spec, s_out_spec, s_shape, acc_args, acc_specs, aliases = _layer_state_io(
        functools.partial(_gla_kernel, c=c, valid=valid, layer=i), n_in, s_all, s_acc, i, H * K, K)
    o, s_new = pl.pallas_call(
        kern,
        grid=(b, n),
        in_specs=[tok(0), tok(1), tok(2), tok(3), s_spec, full2(lb), full2(normw)] + acc_specs,
        out_specs=[pl.BlockSpec((c, W), lambda bb, ii: (bb * n + ii, 0)), s_out_spec],
        out_shape=[jax.ShapeDtypeStruct((t, W), BF16), s_shape],
        input_output_aliases=aliases,
        compiler_params=_params(("arbitrary", "arbitrary")),
        name="gla_scan",
    )(hh, hh, hh, hh, s_in, lb, normw, *acc_args)
    return o, s_new


def _split_bf16(x):
    hi = x.astype(BF16)
    return hi, (x - hi.astype(F32)).astype(BF16)


def _gdn_kernel(qkv_ref, z_ref, ba_ref, aT2_ref, buf_ref, s0_ref, convw_ref,
                dtb_ref, dtb2_ref, alog_ref, alog2_ref, normw_ref,
                o_ref, s_ref, bufout_ref, xpad_ref, *, c, valid, wave):
    HV, HK, K, V = GDN_V_HEADS, GDN_K_HEADS, GDN_HEAD_K, GDN_HEAD_V
    L = 2 * c
    log2c = int(math.log2(c))

    @pl.when(pl.program_id(1) == 0)
    def _():
        s_ref[...] = s0_ref[...]

    _conv_begin(xpad_ref, qkv_ref, buf_ref, c)
    rmask = _iota2((c, 1), 0) < valid

    def conv_silu(col0):
        t = _silu(_conv_tile(xpad_ref, convw_ref, c, col0, K))
        return jnp.where(rmask, t, 0.0) if valid < c else t

    row = _iota2((c, c), 0)
    col = _iota2((c, c), 1)
    ltri = (col <= row).astype(F32)
    rowp = _iota2((c, L), 0)
    lanep = _iota2((c, L), 1)
    left = lanep < c
    sp = jnp.bitwise_and(lanep, c - 1)
    tri2 = sp <= rowp
    strict2 = sp < rowp
    eye2 = (sp == rowp).astype(F32)
    rl = _iota2((L, L), 0)
    cl = _iota2((L, L), 1)
    utri2 = jnp.where(jnp.right_shift(rl, log2c) == jnp.right_shift(cl, log2c),
                      (rl <= cl).astype(F32), 0.0)

    ba = ba_ref[...]
    beta = _sigmoid(ba[:, :HV])
    gl = (-LOG2_E * jnp.exp(alog_ref[...])) * _softplus(ba[:, HV:] + dtb_ref[...])
    glT2 = (-LOG2_E * jnp.exp(alog2_ref[...])) * _softplus(aT2_ref[...] + dtb2_ref[...])
    if valid < c:
        cmask = jnp.bitwise_and(_iota2((1, L), 1), c - 1) < valid
        beta = jnp.where(rmask, beta, 0.0)
        gl = jnp.where(rmask, gl, 0.0)
        glT2 = jnp.where(cmask, glT2, 0.0)
    gcum = _sel_dot(ltri, gl)
    gcumT2 = _dot_sel(glT2, utri2)
    egc = jnp.exp2(gcum)
    glast = gcum[c - 1:c, :]
    elast = jnp.exp2(glast)
    edec = jnp.exp2(glast - gcum)

    n_sq = log2c - 1
    normw = normw_ref[...]
    zeros_cv = jnp.zeros((c, V), F32)

    def pair_cols(a, hk):
        return jnp.where(left, a[:, 2 * hk:2 * hk + 1], a[:, 2 * hk + 1:2 * hk + 2])

    def block_diag(x2):
        return jnp.concatenate([jnp.where(left, x2, 0.0), jnp.where(left, 0.0, x2)], axis=0)

    stack_bf16 = c % (2 * SUBLANES) == 0
    left_b16 = left.astype(BF16)
    right_b16 = 1.0 - left_b16

    def split_pair(x2):
        hi, lo = _split_bf16(x2)
        if stack_bf16:
            bd = tuple(jnp.concatenate([t * left_b16, t * right_b16], axis=0) for t in (hi, lo))
        else:
            bd = _split_bf16(block_diag(x2))
        return (hi, lo), bd

    def block_diag_b16(x2):
        if not stack_bf16:
            return block_diag(x2).astype(BF16)
        xb = x2.astype(BF16)
        return jnp.concatenate([xb * left_b16, xb * right_b16], axis=0)

    def pair_product(lhs_split, rhs_bd):
        lh, ll = lhs_split
        rh, rlo = rhs_bd
        return (jnp.dot(lh, rh, preferred_element_type=F32)
                + jnp.dot(ll, rh, preferred_element_type=F32)
                + jnp.dot(lh, rlo, preferred_element_type=F32))

    for w0 in range(0, HK, wave):
        hks = list(range(w0, min(w0 + wave, HK)))
        qn, kn, gam, qkg, pw, tm = {}, {}, {}, {}, {}, {}
        for hk in hks:
            qh = conv_silu(hk * K)
            kh = conv_silu(GDN_KEY_DIM + hk * K)
            qn[hk] = qh * lax.rsqrt(jnp.sum(qh * qh, axis=-1, keepdims=True) + EPS) * (K ** -0.5)
            kn[hk] = kh * lax.rsqrt(jnp.sum(kh * kh, axis=-1, keepdims=True) + EPS)
            gq = _dot_nt(jnp.concatenate([kn[hk], qn[hk]], axis=0),
                         jnp.concatenate([kn[hk], kn[hk]], axis=0))
            diff = pair_cols(gcum, hk) - gcumT2[hk:hk + 1, :]
            gam[hk] = jnp.where(tri2, jnp.exp2(jnp.where(tri2, diff, 0.0)), 0.0)
            pw[hk] = -jnp.where(strict2, pair_cols(beta, hk) * gq[:c] * gam[hk], 0.0)
            qkg[hk] = gq[c:] * gam[hk]
            tm[hk] = eye2 + pw[hk]
        pw_split = {hk: split_pair(pw[hk]) for hk in hks}
        for _ in range(n_sq):
            for hk in hks:
                pw_split[hk] = split_pair(pair_product(*pw_split[hk]))
            for hk in hks:
                tm[hk] = tm[hk] + pair_product(_split_bf16(tm[hk]), pw_split[hk][1])
        wu, ws_qs, v_new = {}, {}, {}
        for hk in hks:
            rhs = []
            for j in range(2):
                h = 2 * hk + j
                bh = beta[:, h:h + 1]
                vh = conv_silu(2 * GDN_KEY_DIM + h * V)
                rhs.append(jnp.concatenate([kn[hk] * (bh * egc[:, h:h + 1]), vh * bh], axis=1))
            wu[hk] = _dot(block_diag_b16(tm[hk]), jnp.concatenate(rhs, axis=0))
        for hk in hks:
            for j in range(2):
                h = 2 * hk + j
                wq = jnp.concatenate([wu[hk][j * c:(j + 1) * c, :K], qn[hk] * egc[:, h:h + 1]], axis=0)
                ws_qs[h] = _dot(wq, s_ref[h * K:(h + 1) * K, :])
                v_new[h] = wu[hk][j * c:(j + 1) * c, K:] - ws_qs[h][:c]
        for hk in hks:
            ha, hb = 2 * hk, 2 * hk + 1
            y2 = _dot(block_diag_b16(qkg[hk]), jnp.concatenate([v_new[ha], v_new[hb]], axis=0))
            kdec = jnp.concatenate([kn[hk] * edec[:, ha:ha + 1], kn[hk] * edec[:, hb:hb + 1]], axis=0)
            vbd = jnp.concatenate([jnp.concatenate([v_new[ha], zeros_cv], axis=1),
                                   jnp.concatenate([zeros_cv, v_new[hb]], axis=1)], axis=0)
            upd = _dot_tn(kdec, vbd)
            for j, h in enumerate((ha, hb)):
                sl = slice(h * K, (h + 1) * K)
                s_ref[sl, :] = elast[:, h:h + 1] * s_ref[sl, :] + upd[:, j * V:(j + 1) * V]
                y = ws_qs[h][c:] + y2[j * c:(j + 1) * c]
                yn = (y * lax.rsqrt(jnp.mean(y * y, axis=-1, keepdims=True) + EPS)) * normw
                o_ref[:, h * V:(h + 1) * V] = (yn * _silu(z_ref[:, h * V:(h + 1) * V])).astype(o_ref.dtype)
    _conv_carry(xpad_ref, bufout_ref, c, valid)


def _pair_packed_transpose(a, b, n, c):
    hh = a.shape[-1] // 2
    x = a.reshape(b, n, c, hh, 2)
    return jnp.transpose(x, (0, 1, 3, 4, 2)).reshape(b, n, hh, 2 * c)


def _gdn(qkv, z, ba, buf, s_all, s_acc, p, i, *, b, n, c, valid, row0):
    t, blk0 = b * n * c, row0 // c
    HV, HK, K, V = GDN_V_HEADS, GDN_K_HEADS, GDN_HEAD_K, GDN_HEAD_V
    C = GDN_CONV_DIM
    tok = lambda w: pl.BlockSpec((c, w), lambda bb, ii: (blk0 + bb * n + ii, 0))
    tok_out = lambda w: pl.BlockSpec((c, w), lambda bb, ii: (bb * n + ii, 0))
    full2 = lambda a: pl.BlockSpec(a.shape, lambda bb, ii: (0, 0))
    aT2 = _pair_packed_transpose(ba[row0:row0 + t, HV:], b, n, c)
    convw = p["gdn_conv_w"][i]
    dtb = p["gdn_dt_bias"][i].reshape(1, HV)
    dtb2 = jnp.repeat(p["gdn_dt_bias"][i].reshape(HK, 2), c, axis=1)
    alog = p["gdn_A_log"][i].reshape(1, HV)
    alog2 = jnp.repeat(p["gdn_A_log"][i].reshape(HK, 2), c, axis=1)
    normw = p["gdn_norm_w"][i].reshape(1, V)
    small = [convw, dtb, dtb2, alog, alog2, normw]
    n_in = 6 + len(small)
    wave = GDN_WAVE if c >= CHUNK else GDN_WAVE_SHORT_CHUNK
    kern, s_in, s_spec, s_out_spec, s_shape, acc_args, acc_specs, aliases = _layer_state_io(
        functools.partial(_gdn_kernel, c=c, valid=valid, wave=wave), n_in, s_all, s_acc, i, HV * K, V)
    o, s_new, buf_new = pl.pallas_call(
        kern,
        grid=(b, n),
        in_specs=[tok(C), tok(GDN_VAL_DIM), tok(2 * HV),
                  pl.BlockSpec((None, None, HK, 2 * c), lambda bb, ii: (bb, ii, 0, 0)),
                  pl.BlockSpec((None, CONV_W - 1, C), lambda bb, ii: (bb, 0, 0)),
                  s_spec]
                 + [full2(a) for a in small] + acc_specs,
        out_specs=[tok_out(GDN_VAL_DIM), s_out_spec,
                   pl.BlockSpec((None, CONV_W - 1, C), lambda bb, ii: (bb, 0, 0))],
        out_shape=[jax.ShapeDtypeStruct((t, GDN_VAL_DIM), BF16), s_shape,
                   jax.ShapeDtypeStruct((b, CONV_W - 1, C), F32)],
        scratch_shapes=[pltpu.VMEM((8 + c, C), F32)],
        input_output_aliases=aliases,
        compiler_params=_params(("arbitrary", "arbitrary")),
        name="gdn_scan",
    )(qkv, z, ba, aT2, buf, s_in, *small, *acc_args)
    return o, s_new, buf_new


def _trunk(groups, p):
    d = groups[0]["x"].shape[-1]
    row0 = 0
    for gr in groups:
        b, lp, _ = gr["x"].shape
        gr["dims"] = dict(b=b, n=lp // gr["c"], c=gr["c"], valid=gr["valid"], row0=row0)
        gr["rows"] = b * lp
        gr["new"] = dict(ssm=None, hg=None, gdn=None, ssm_conv=[], gdn_conv=[])
        row0 += b * lp
    xt = jnp.concatenate([gr["x"].reshape(gr["rows"], d) for gr in groups], axis=0)
    for l in range(DEPTH):
        i = l // 2
        if l % 2 == 0:
            h, dt = _rmsnorm_proj(xt, p["norm_mix"][l], p["w_in_ab_t"], i, AB_OFF_DT, SSM_HEADS)
            in_proj = functools.partial(_matmul, h, p["w_in_ab_t"], i, w_rows_are_outputs=True)
            z = in_proj(0, SSM_D_INNER)
            xbc = in_proj(AB_OFF_XBC, SSM_CONV_DIM)
            hh = in_proj(AB_OFF_HG, 4 * HG_WIDTH)
            ys, os = [], []
            for gr in groups:
                new = gr["new"]
                y, new["ssm"], b1 = _ssd(z, xbc, dt, gr["ssm_conv"][i], gr["ssm"], new["ssm"],
                                         p, i, **gr["dims"])
                o, new["hg"] = _gla(hh, gr["hg"], new["hg"], p, i, **gr["dims"])
                new["ssm_conv"].append(b1)
                ys.append(y)
                os.append(o)
            xt = _matmul((ys, os), p["w_out_ab"], i, 0, D_MODEL, epilogue="residual", residual=xt)
        else:
            h, ba = _rmsnorm_proj(xt, p["norm_mix"][l], p["w_in_c_t"], i, C_OFF_BA, 2 * GDN_V_HEADS)
            in_proj = functools.partial(_matmul, h, p["w_in_c_t"], i, w_rows_are_outputs=True)
            qkv = in_proj(0, GDN_CONV_DIM)
            z = in_proj(C_OFF_Z, GDN_VAL_DIM)
            os = []
            for gr in groups:
                new = gr["new"]
                o, new["gdn"], b3 = _gdn(qkv, z, ba, gr["gdn_conv"][i], gr["gdn"], new["gdn"],
                                         p, i, **gr["dims"])
                new["gdn_conv"].append(b3)
                os.append(o)
            xt = _matmul((os,), p["w_out_c"], i, 0, D_MODEL, epilogue="residual", residual=xt)
        h = _rmsnorm(xt, p["norm_mlp"][l], BF16)
        u = _matmul(h, p["w_ff1"], l, 0, FFN_HIDDEN, epilogue="relu2", out_dtype=BF16)
        xt = _matmul(u, p["w_ff2"], l, 0, D_MODEL, epilogue="residual", residual=xt)
    outs = []
    for gr in groups:
        new = gr["new"]
        y = _rmsnorm(xt, p["norm_final"], F32, gr["dims"]["row0"], gr["rows"]).reshape(gr["x"].shape)
        outs.append((y, new["ssm"].reshape(gr["ssm"].shape), jnp.stack(new["ssm_conv"]),
                     new["hg"].reshape(gr["hg"].shape), new["gdn"].reshape(gr["gdn"].shape),
                     jnp.stack(new["gdn_conv"])))
    return outs


def kernel(x_prompt, x_sample, state_ssm, state_ssm_conv, state_hgrn, state_gdn, state_gdn_conv, norm_mix, norm_mlp, norm_final, w_in_ab, ssm_conv_w, ssm_conv_b, ssm_dt_bias, ssm_A_log, ssm_D, ssm_norm_w, hg_lower_bounds, hg_norm_w, w_out_ab, w_in_c, gdn_conv_w, gdn_dt_bias, gdn_A_log, gdn_norm_w, w_out_c, w_ff1, w_ff2):
    p = dict(norm_mix=norm_mix, norm_mlp=norm_mlp, norm_final=norm_final, w_in_ab=w_in_ab,
             ssm_conv_w=ssm_conv_w, ssm_conv_b=ssm_conv_b, ssm_dt_bias=ssm_dt_bias, ssm_A_log=ssm_A_log,
             ssm_D=ssm_D, ssm_norm_w=ssm_norm_w, hg_lower_bounds=hg_lower_bounds, hg_norm_w=hg_norm_w,
             w_out_ab=w_out_ab, w_in_c=w_in_c, gdn_conv_w=gdn_conv_w, gdn_dt_bias=gdn_dt_bias,
             gdn_A_log=gdn_A_log, gdn_norm_w=gdn_norm_w, w_out_c=w_out_c, w_ff1=w_ff1, w_ff2=w_ff2)
    p["w_in_ab_t"] = jnp.swapaxes(w_in_ab, 1, 2)
    p["w_in_c_t"] = jnp.swapaxes(w_in_c, 1, 2)

    bp, seq, _ = x_prompt.shape
    z_ssm = jnp.zeros((N_AB, bp, SSM_HEADS, SSM_HEAD_DIM, SSM_D_STATE), F32)
    z_ssm_conv = jnp.zeros((N_AB, bp, CONV_W - 1, SSM_CONV_DIM), F32)
    z_hg = jnp.zeros((N_AB, bp, HG_HEADS, HG_HEAD_DIM, HG_HEAD_DIM), F32)
    z_gdn = jnp.zeros((N_C, bp, GDN_V_HEADS, GDN_HEAD_K, GDN_HEAD_V), F32)
    z_gdn_conv = jnp.zeros((N_C, bp, CONV_W - 1, GDN_CONV_DIM), F32)
    c_p = min(CHUNK, seq)
    prompt = dict(x=x_prompt, c=c_p, valid=c_p, ssm=z_ssm, ssm_conv=z_ssm_conv, hg=z_hg, gdn=z_gdn,
                  gdn_conv=z_gdn_conv)

    bs, dec, _ = x_sample.shape
    c_s = -(-dec // SUBLANES) * SUBLANES
    xs_pad = jnp.pad(x_sample, ((0, 0), (0, c_s - dec), (0, 0)))
    sample = dict(x=xs_pad, c=c_s, valid=dec, ssm=state_ssm, ssm_conv=state_ssm_conv, hg=state_hgrn,
                  gdn=state_gdn, gdn_conv=state_gdn_conv)
    outs_p, outs_s = _trunk([prompt, sample], p)
    y_sample = outs_s[0][:, :dec]
    return (outs_p[0], y_sample) + tuple(outs_p[1:]) + tuple(outs_s[1:])
```

```python
import functools
import math

import jax
import jax.numpy as jnp
from jax import lax
from jax.experimental import pallas as pl
from jax.experimental.pallas import tpu as pltpu

F32 = jnp.float32
BF16 = jnp.bfloat16
HIGHEST = lax.Precision.HIGHEST

D_MODEL = 2048
DEPTH = 4
N_AB = (DEPTH + 1) // 2
N_C = DEPTH // 2
CONV_W = 4
CHUNK = 64
EPS = 1e-6
TINY = 1e-30
LOG2_E = math.log2(math.e)

SSM_D_INNER = D_MODEL
SSM_HEAD_DIM = 64
SSM_HEADS = SSM_D_INNER // SSM_HEAD_DIM
SSM_GROUPS = 4
SSM_D_STATE = 128
SSM_CONV_DIM = SSM_D_INNER + 2 * SSM_GROUPS * SSM_D_STATE

HG_WIDTH = D_MODEL
HG_HEAD_DIM = 128
HG_HEADS = HG_WIDTH // HG_HEAD_DIM

GDN_HEAD_K = 128
GDN_HEAD_V = 128
GDN_K_HEADS = D_MODEL // GDN_HEAD_K
GDN_V_HEADS = 2 * GDN_K_HEADS
GDN_KEY_DIM = GDN_K_HEADS * GDN_HEAD_K
GDN_VAL_DIM = GDN_V_HEADS * GDN_HEAD_V
GDN_CONV_DIM = 2 * GDN_KEY_DIM + GDN_VAL_DIM

FFN_HIDDEN = 4 * D_MODEL

AB_OFF_XBC = SSM_D_INNER
AB_OFF_DT = AB_OFF_XBC + SSM_CONV_DIM
AB_OFF_HG = AB_OFF_DT + SSM_HEADS
C_OFF_Z = GDN_CONV_DIM
C_OFF_BA = C_OFF_Z + GDN_VAL_DIM

SUBLANES = 8
VMEM_LIMIT = 52 * 1024 * 1024
GDN_WAVE = 8
GDN_WAVE_SHORT_CHUNK = 16
GLA_WAVE = 16
SSD_WAVE = 4


def _params(sem):
    return pltpu.CompilerParams(dimension_semantics=sem, vmem_limit_bytes=VMEM_LIMIT)


def _dot(a, b):
    return jnp.dot(a.astype(BF16), b.astype(BF16), preferred_element_type=F32)


def _dot_nt(a, b):
    return lax.dot_general(a.astype(BF16), b.astype(BF16), (((1,), (1,)), ((), ())),
                           preferred_element_type=F32)


def _dot_tn(a, b):
    return lax.dot_general(a.astype(BF16), b.astype(BF16), (((0,), (0,)), ((), ())),
                           preferred_element_type=F32)


def _split3_bf16(x):
    p0 = x.astype(BF16)
    r1 = x - p0.astype(F32)
    p1 = r1.astype(BF16)
    return p0, p1, (r1 - p1.astype(F32)).astype(BF16)


def _sel_dot(sel, x):
    s = sel.astype(BF16)
    p0, p1, p2 = _split3_bf16(x)
    return (jnp.dot(s, p0, preferred_element_type=F32) + jnp.dot(s, p1, preferred_element_type=F32)
            + jnp.dot(s, p2, preferred_element_type=F32))


def _dot_sel(x, sel):
    s = sel.astype(BF16)
    p0, p1, p2 = _split3_bf16(x)
    return (jnp.dot(p0, s, preferred_element_type=F32) + jnp.dot(p1, s, preferred_element_type=F32)
            + jnp.dot(p2, s, preferred_element_type=F32))


def _sigmoid(x):
    return 1.0 / (1.0 + jnp.exp2(x * (-LOG2_E)))


def _silu(x):
    return x * _sigmoid(x)


def _softplus(x):
    return jnp.maximum(x, 0.0) + jnp.log1p(jnp.exp(-jnp.abs(x)))


def _iota2(shape, dim):
    return lax.broadcasted_iota(jnp.int32, shape, dim)


def _block_mid_rows(g, m, rowv):
    c, w = g.shape
    if 2 * m >= SUBLANES:
        return jnp.concatenate([jnp.broadcast_to(g[b0 + m:b0 + m + 1, :], (2 * m, w))
                                for b0 in range(0, c, 2 * m)], axis=0)
    pos = jnp.bitwise_and(rowv, 2 * m - 1)
    out = g
    for off in range(1 - m, m + 1):
        if off != 0:
            out = jnp.where(pos == m - off, pltpu.roll(g, (-off) % c, 0), out)
    return out


def _conv_begin(xpad_ref, x_ref, buf_ref, c):
    @pl.when(pl.program_id(1) == 0)
    def _():
        xpad_ref[5:8, :] = buf_ref[...]

    xpad_ref[8:8 + c, :] = x_ref[...]


def _conv_tile(xpad_ref, w_ref, c, col0, width):
    sl = slice(col0, col0 + width)
    y = xpad_ref[5:5 + c, sl] * w_ref[0:1, sl]
    for j in range(1, CONV_W):
        y = y + xpad_ref[5 + j:5 + j + c, sl] * w_ref[j:j + 1, sl]
    return y


def _conv_carry(xpad_ref, bufout_ref, c, valid_last):
    @pl.when(pl.program_id(1) == pl.num_programs(1) - 1)
    def _():
        bufout_ref[...] = xpad_ref[8 + valid_last - 3:8 + valid_last, :]

    xpad_ref[5:8, :] = xpad_ref[8 + c - 3:8 + c, :]


def _rmsnorm_kernel(x_ref, w_ref, o_ref):
    x = x_ref[...]
    ms = jnp.mean(x * x, axis=-1, keepdims=True)
    o_ref[...] = ((x * lax.rsqrt(ms + EPS)) * w_ref[...]).astype(o_ref.dtype)


def _rmsnorm(x, w, out_dtype, row0=0, rows=None):
    d = x.shape[1]
    t = x.shape[0] if rows is None else rows
    tm = min(512, t)
    blk0 = row0 // tm
    return pl.pallas_call(
        _rmsnorm_kernel,
        grid=(t // tm,),
        in_specs=[pl.BlockSpec((tm, d), lambda i: (blk0 + i, 0)),
                  pl.BlockSpec((1, d), lambda i: (0, 0))],
        out_specs=pl.BlockSpec((tm, d), lambda i: (i, 0)),
        out_shape=jax.ShapeDtypeStruct((t, d), out_dtype),
        compiler_params=_params(("arbitrary",)),
        name="rmsnorm",
    )(x, w.reshape(1, d))


def _rmsnorm_proj_kernel(x_ref, w_ref, wp_ref, o_ref, p_ref):
    x = x_ref[...]
    ms = jnp.mean(x * x, axis=-1, keepdims=True)
    h = ((x * lax.rsqrt(ms + EPS)) * w_ref[...]).astype(o_ref.dtype)
    o_ref[...] = h
    p_ref[...] = lax.dot_general(h, wp_ref[...].astype(BF16), (((1,), (1,)), ((), ())),
                                 preferred_element_type=F32)


def _rmsnorm_proj(x, w, w_t, layer, col0, n):
    t, d = x.shape
    tm = min(512, t)
    n_total = w_t.shape[1]
    row0 = layer * n_total + col0
    return pl.pallas_call(
        _rmsnorm_proj_kernel,
        grid=(t // tm,),
        in_specs=[pl.BlockSpec((tm, d), lambda i: (i, 0)),
                  pl.BlockSpec((1, d), lambda i: (0, 0)),
                  pl.BlockSpec((pl.Element(n), pl.Element(d)), lambda i: (row0, 0))],
        out_specs=[pl.BlockSpec((tm, d), lambda i: (i, 0)),
                   pl.BlockSpec((tm, n), lambda i: (i, 0))],
        out_shape=[jax.ShapeDtypeStruct((t, d), BF16), jax.ShapeDtypeStruct((t, n), F32)],
        compiler_params=_params(("arbitrary",)),
        name="rmsnorm_proj",
    )(x, w.reshape(1, d), w_t.reshape(w_t.shape[0] * n_total, d))


def _mm_kernel(*refs, n_seg, group_tiles, epilogue, w_rows_are_outputs):
    n_a = n_seg * len(group_tiles)
    a_refs, refs = refs[:n_a], refs[n_a:]
    if epilogue == "residual":
        w_ref, r_ref, o_ref, wb_ref = refs
    else:
        w_ref, o_ref, wb_ref = refs
    i = pl.program_id(1)

    @pl.when(i == 0)
    def _():
        wb_ref[...] = w_ref[...].astype(BF16)

    def row_tile(seg_refs):
        if w_rows_are_outputs:
            acc = lax.dot_general(seg_refs[0][...], wb_ref[...], (((1,), (1,)), ((), ())),
                                  preferred_element_type=F32)
        else:
            acc, k0 = None, 0
            for a_ref in seg_refs:
                k1 = k0 + a_ref.shape[1]
                part = jnp.dot(a_ref[...], wb_ref[k0:k1, :], preferred_element_type=F32)
                acc, k0 = (part if acc is None else acc + part), k1
        if epilogue == "relu2":
            acc = jnp.square(jnp.maximum(acc, 0.0))
        elif epilogue == "residual":
            acc = r_ref[...] + acc
        o_ref[...] = acc.astype(o_ref.dtype)

    tile0 = 0
    for g, n_tiles in enumerate(group_tiles):
        seg_refs = a_refs[g * n_seg:(g + 1) * n_seg]
        if len(group_tiles) == 1:
            row_tile(seg_refs)
        else:
            pl.when(jnp.logical_and(i >= tile0, i < tile0 + n_tiles))(
                functools.partial(row_tile, seg_refs))
        tile0 += n_tiles


def _mm_tiles(t, k, n):
    tn ={2048: 1024, 4096: 512, 8192: 512}[k]
    tm = {2048: 1024, 4096: 1024, 8192: 512}[k]
    return tm, min(tn, n), (2 if k == 2048 else 1)


def _matmul(a, w3, layer, col0, n, *, epilogue="none", residual=None, out_dtype=F32,
            w_rows_are_outputs=False):
    a_segs = a if isinstance(a, (tuple, list)) else (a,)
    a_segs = [seg if isinstance(seg, (tuple, list)) else (seg,) for seg in a_segs]
    group_rows = [arr.shape[0] for arr in a_segs[0]]
    t, k = sum(group_rows), sum(seg[0].shape[1] for seg in a_segs)
    tm, tn, w_bufs = _mm_tiles(t, k, n)
    while any(r % tm for r in group_rows):
        tm //= 2
    group_tiles = tuple(r // tm for r in group_rows)
    tile0 = [sum(group_tiles[:g]) for g in range(len(group_tiles))]

    def a_spec(g, width):
        return pl.BlockSpec((tm, width),
                            lambda j, i: (jnp.clip(i - tile0[g], 0, group_tiles[g] - 1), 0))
    if w_rows_are_outputs:
        n_total = w3.shape[1]
        row0 = layer * n_total + col0
        w3 = w3.reshape(w3.shape[0] * n_total, k)
        w_spec = pl.BlockSpec((pl.Element(tn), pl.Element(k)), lambda j, i: (pl.multiple_of(row0 + j * tn, SUBLANES), 0),
                              pipeline_mode=pl.Buffered(w_bufs))
        wb_shape = (tn, k)
    else:
        cb0 = col0 // tn
        w_spec = pl.BlockSpec((None, k, tn), lambda j, i: (layer, 0, cb0 + j),
                              pipeline_mode=pl.Buffered(w_bufs))
        wb_shape = (k, tn)
    in_specs, args = [], []
    for g in range(len(group_tiles)):
        for seg in a_segs:
            in_specs.append(a_spec(g, seg[g].shape[1]))
            args.append(seg[g])
    in_specs.append(w_spec)
    args.append(w3)
    if epilogue == "residual":
        in_specs.append(pl.BlockSpec((tm, tn), lambda j, i: (i, j)))
        args.append(residual)
    return pl.pallas_call(
        functools.partial(_mm_kernel, n_seg=len(a_segs), group_tiles=group_tiles, epilogue=epilogue,
                          w_rows_are_outputs=w_rows_are_outputs),
        grid=(n // tn, t // tm),
        in_specs=in_specs,
        out_specs=pl.BlockSpec((tm, tn), lambda j, i: (i, j)),
        out_shape=jax.ShapeDtypeStruct((t, n), out_dtype),
        scratch_shapes=[pltpu.VMEM(wb_shape, BF16)],
        compiler_params=_params(("arbitrary", "arbitrary")),
        name="matmul_" + epilogue,
    )(*args)


def _ssd_kernel(z_ref, xbc_ref, dt_ref, dtT2_ref, buf_ref, s0_ref, convw_ref, convb_ref,
                dtb_ref, dtb2_ref, alog_ref, alog2_ref, dskip_ref, normw_ref,
                y_ref, s_ref, bufout_ref, xpad_ref, *, c, valid):
    H, P, N, G = SSM_HEADS, SSM_HEAD_DIM, SSM_D_STATE, SSM_GROUPS
    W = 2 * P
    pairs_per_group = H // G // 2
    L = 2 * c
    log2c = int(math.log2(c))

    @pl.when(pl.program_id(1) == 0)
    def _():
        s_ref[...] = s0_ref[...]

    _conv_begin(xpad_ref, xbc_ref, buf_ref, c)
    rmask = _iota2((c, 1), 0) < valid

    def conv_silu(col0):
        t = _silu(_conv_tile(xpad_ref, convw_ref, c, col0, W) + convb_ref[:, col0:col0 + W])
        return jnp.where(rmask, t, 0.0) if valid < c else t

    row = _iota2((c, c), 0)
    col = _iota2((c, c), 1)
    ltri = (col <= row).astype(F32)
    rowp = _iota2((c, L), 0)
    lanep = _iota2((c, L), 1)
    tri2 = jnp.bitwise_and(lanep, c - 1) <= rowp
    rl = _iota2((L, L), 0)
    cl = _iota2((L, L), 1)
    utri2 = jnp.where(jnp.right_shift(rl, log2c) == jnp.right_shift(cl, log2c),
                      (rl <= cl).astype(F32), 0.0)
    first =_iota2((c, W), 1) < P
    first_row = _iota2((W, N), 0) < P

    dt = _softplus(dt_ref[...] + dtb_ref[...])
    la = dt * (-jnp.exp(alog_ref[...]))
    laT2 = _softplus(dtT2_ref[...] + dtb2_ref[...]) * (-jnp.exp(alog2_ref[...]))
    if valid < c:
        cmask = jnp.bitwise_and(_iota2((1, L), 1), c - 1) < valid
        la = jnp.where(rmask, la, 0.0)
        laT2 = jnp.where(cmask, laT2, 0.0)
        dt = jnp.where(rmask, dt, 0.0)
    acum = _sel_dot(ltri, la)
    acumT2 = _dot_sel(laT2, utri2)
    alast = acum[c - 1:c, :]
    elast = jnp.exp(alast)

    ch_head = jnp.right_shift(_iota2((H, SSM_D_INNER), 1), int(math.log2(P)))
    expand = (ch_head == _iota2((H, SSM_D_INNER), 0)).astype(F32)
    lane2 = _iota2((H, (H // 2) * L), 1)
    pk_head = 2 * jnp.right_shift(lane2, log2c + 1) + jnp.bitwise_and(jnp.right_shift(lane2, log2c), 1)
    expand2 = (pk_head == _iota2((H, (H // 2) * L), 0)).astype(F32)
    dt_x = _dot_sel(dt, expand)
    ea_x = jnp.exp(_dot_sel(acum, expand))
    ew_x = jnp.exp(_dot_sel(alast - acum, expand))
    dskip_x = _dot_sel(jnp.broadcast_to(dskip_ref[...], (SUBLANES, H)), expand)[0:1, :]
    acum_x2 = _dot_sel(acum, expand2)

    for g0 in range(0, G, SSD_WAVE):
        groups = range(g0, min(g0 + SSD_WAVE, G))
        prs = [g * pairs_per_group + j for g in groups for j in range(pairs_per_group)]
        bg, cg, cb2, xs, xdt, m2, xbd, xw = {}, {}, {}, {}, {}, {}, {}, {}
        for g in groups:
            bg[g] = conv_silu(SSM_D_INNER + g * N)
            cg[g] = conv_silu(SSM_D_INNER + G * N + g * N)
            cb2[g] = _dot_nt(cg[g], jnp.concatenate([bg[g], bg[g]], axis=0))
        for pr in prs:
            sl = slice(pr * W, (pr + 1) * W)
            xs[pr] = conv_silu(pr * W)
            xdt[pr] = xs[pr] * dt_x[:, sl]
            diff = acum_x2[:, pr * L:(pr + 1) * L] - acumT2[pr:pr + 1, :]
            decay = jnp.where(tri2, jnp.exp(jnp.where(tri2, diff, 0.0)), 0.0)
            m2[pr] = (cb2[pr // pairs_per_group] * decay).astype(BF16)
            xbd[pr] = jnp.concatenate([jnp.where(first, xdt[pr], 0.0),
                                       jnp.where(first, 0.0, xdt[pr])], axis=0).astype(BF16)
            xw[pr] = (xdt[pr] * ew_x[:, sl]).astype(BF16)
        y_intra, y_inter, upd = {}, {}, {}
        for pr in prs:
            g = pr // pairs_per_group
            sl = slice(pr * W, (pr + 1) * W)
            y_intra[pr] = _dot(m2[pr], xbd[pr])
            y_inter[pr] = _dot_nt(cg[g], s_ref[sl, :])
            upd[pr] = _dot_tn(xw[pr], bg[g])
        gated, sumsq = {}, {}
        for pr in prs:
            g = pr // pairs_per_group
            sl = slice(pr * W, (pr + 1) * W)
            y = y_intra[pr] + y_inter[pr] * ea_x[:, sl] + dskip_x[:, sl] * xs[pr]
            scale = jnp.where(first_row, elast[:, 2 * pr:2 * pr + 1], elast[:, 2 * pr + 1:2 * pr + 2])
            s_ref[sl, :] = scale * s_ref[sl, :] + upd[pr]
            y = y * _silu(z_ref[:, sl])
            ss = jnp.sum(y * y, axis=-1, keepdims=True)
            sumsq[g] = ss if g not in sumsq else sumsq[g] + ss
            gated[pr] = y
        for pr in prs:
            sl = slice(pr * W, (pr + 1) * W)
            rs = lax.rsqrt(sumsq[pr // pairs_per_group] * (1.0 / (pairs_per_group * W)) + EPS)
            y_ref[:, sl] = ((gated[pr] * rs) * normw_ref[:, sl]).astype(y_ref.dtype)
    _conv_carry(xpad_ref, bufout_ref, c, valid)


def _layer_state_io(kern, n_in, s_all, s_acc, layer, rows, cols):
    n_layers, b = s_all.shape[:2]
    first = s_acc is None
    in_spec = pl.BlockSpec((None, None, rows, cols), lambda bb, ii: (layer, bb, 0, 0))
    out_spec = (pl.BlockSpec((n_layers, None, rows, cols), lambda bb, ii: (0, bb, 0, 0)) if first
                else in_spec)
    shape = jax.ShapeDtypeStruct((n_layers, b, rows, cols), F32)
    s_pos = n_in + 1

    def adapted(*refs):
        refs = list(refs) if first else list(refs[:n_in]) + list(refs[n_in + 1:])
        if first:
            s_full = refs[s_pos]
            refs[s_pos] = s_full.at[layer]

            @pl.when(pl.program_id(1) == 0)
            def _():
                for other in range(n_layers):
                    if other != layer:
                        s_full[other] = jnp.zeros((rows, cols), F32)
        return kern(*refs)

    extra_args = [] if first else [s_acc]
    extra_specs = [] if first else [pl.BlockSpec(memory_space=pl.ANY)]
    aliases = {} if first else {n_in: 1}
    return (adapted, s_all.reshape(n_layers, b, rows, cols), in_spec, out_spec, shape,
            extra_args, extra_specs, aliases)


def _ssd(z, xbc, dt, buf, s_all, s_acc, p, i, *, b, n, c, valid, row0):
    t, blk0 = b * n * c, row0 // c
    H, P, N = SSM_HEADS, SSM_HEAD_DIM, SSM_D_STATE
    C = SSM_CONV_DIM
    tok = lambda w: pl.BlockSpec((c, w), lambda bb, ii: (blk0 + bb * n + ii, 0))
    tok_out = lambda w: pl.BlockSpec((c, w), lambda bb, ii: (bb * n + ii, 0))
    full2 = lambda a: pl.BlockSpec(a.shape, lambda bb, ii: (0, 0))
    dtT2 = _pair_packed_transpose(dt[row0:row0 + t], b, n, c)
    convw = p["ssm_conv_w"][i]
    convb = p["ssm_conv_b"][i].reshape(1, C)
    dtb = p["ssm_dt_bias"][i].reshape(1, H)
    dtb2 = jnp.repeat(p["ssm_dt_bias"][i].reshape(H // 2, 2), c, axis=1)
    alog = p["ssm_A_log"][i].reshape(1, H)
    alog2 = jnp.repeat(p["ssm_A_log"][i].reshape(H // 2, 2), c, axis=1)
    dskip = p["ssm_D"][i].reshape(1, H)
    normw = p["ssm_norm_w"][i].reshape(1, SSM_D_INNER)
    small = [convw, convb, dtb, dtb2, alog, alog2, dskip, normw]
    n_in = 6 + len(small)
    kern, s_in, s_spec, s_out_spec, s_shape, acc_args, acc_specs, aliases = _layer_state_io(
        functools.partial(_ssd_kernel, c=c, valid=valid), n_in, s_all, s_acc, i, H * P, N)
    y, s_new, buf_new = pl.pallas_call(
        kern,
        grid=(b, n),
        in_specs=[tok(SSM_D_INNER), tok(C), tok(H),
                  pl.BlockSpec((None, None, H // 2, 2 * c), lambda bb, ii: (bb, ii, 0, 0)),
                  pl.BlockSpec((None, CONV_W - 1, C), lambda bb, ii: (bb, 0, 0)),
                  s_spec]
                 + [full2(a) for a in small] + acc_specs,
        out_specs=[tok_out(SSM_D_INNER), s_out_spec,
                   pl.BlockSpec((None, CONV_W - 1, C), lambda bb, ii: (bb, 0, 0))],
        out_shape=[jax.ShapeDtypeStruct((t, SSM_D_INNER), BF16), s_shape,
                   jax.ShapeDtypeStruct((b, CONV_W - 1, C), F32)],
        scratch_shapes=[pltpu.VMEM((8 + c, C), F32)],
        input_output_aliases=aliases,
        compiler_params=_params(("arbitrary", "arbitrary")),
        name="ssd_scan",
    )(z, xbc, dt, dtT2, buf, s_in, *small, *acc_args)
    return y, s_new, buf_new


def _gla_kernel(hq_ref, hf_ref, hi_ref, hg_ref, s0_ref, lb_ref, normw_ref,
                o_ref, s_ref, *, c, valid, layer):
    H, K = HG_HEADS, HG_HEAD_DIM

    @pl.when(pl.program_id(1) == 0)
    def _():
        s_ref[...] = s0_ref[...]

    lbp = lb_ref[...]
    mx = lbp[0:1, :]
    for j in range(1, N_AB):
        mx = jnp.maximum(mx, lbp[j:j + 1, :])
    e = [jnp.exp(lbp[j:j + 1, :] - mx) for j in range(N_AB)]
    den = e[0]
    for j in range(1, N_AB):
        den = den + e[j]
    lb = e[0] / den
    for j in range(1, layer + 1):
        lb = lb + e[j] / den
    lb = lb - e[0] / den

    rmask = _iota2((c, 1), 0) < valid
    one_m_lb = 1.0 - lb

    def masked(t):
        return jnp.where(rmask, t, 0.0) if valid < c else t

    logf, ks = [], []
    for h in range(H):
        sl = slice(h * K, (h + 1) * K)
        sig = _sigmoid(hf_ref[:, sl])
        logf.append(masked(jnp.log(jnp.maximum(lb[:, sl] + one_m_lb[:, sl] * sig, TINY))))
        ks.append(masked(one_m_lb[:, sl] * (1.0 - sig)))

    row = _iota2((c, c), 0)
    col = _iota2((c, c), 1)
    ltri = (col <= row).astype(F32)
    g_all = _sel_dot(ltri, jnp.concatenate(logf, axis=1)) * LOG2_E

    rowv = _iota2((c, 1), 0)
    eye = row == col
    levels = []
    m = c // 2
    while m >= 1:
        sh2m = int(math.log2(2 * m))
        t_up = jnp.bitwise_and(row, 2 * m - 1) - m
        s_lo = (m - 1) - jnp.bitwise_and(col, 2 * m - 1)
        other_blk = -jnp.bitwise_xor(jnp.right_shift(row, sh2m), jnp.right_shift(col, sh2m))
        upper = jnp.bitwise_and(rowv, 2 * m - 1) >= m
        levels.append((m, upper, jnp.where(upper, 1.0, -1.0),
                       jnp.bitwise_or(jnp.bitwise_or(t_up, s_lo), other_blk) >= 0))
        m //= 2

    for h0 in range(0, H, GLA_WAVE):
        heads = range(h0, min(h0 + GLA_WAVE, H))
        att, qs, vs = {}, {}, {}
        for h in heads:
            sl = slice(h * K, (h + 1) * K)
            g = g_all[:, sl]
            q = masked(_silu(hq_ref[:, sl]))
            k = ks[h]
            a = jnp.where(eye, _dot_nt(q, k), 0.0)
            for m, upper, sign, pair_mask in levels:
                d = (g - _block_mid_rows(g, m, rowv)) * sign
                zl = (jnp.where(upper, q, k) * jnp.exp2(d)).astype(BF16)
                a = a + jnp.where(pair_mask, _dot_nt(zl, zl), 0.0)
            att[h], qs[h], vs[h] = a, q, masked(hi_ref[:, sl]).astype(BF16)
        for h in heads:
            sl = slice(h * K, (h + 1) * K)
            g = g_all[:, sl]
            glast = g[c - 1:c, :]
            sh = s_ref[sl, :]
            y = _dot(att[h], vs[h]) + _dot(qs[h] * jnp.exp2(g), sh)
            dec_col = jnp.exp2(jnp.broadcast_to(glast, (SUBLANES, K)).T[:, 0:1])
            s_ref[sl, :] = dec_col * sh + _dot_tn(ks[h] * jnp.exp2(glast - g), vs[h])
            yn = y * lax.rsqrt(jnp.mean(y * y, axis=-1, keepdims=True) + EPS)
            o_ref[:, sl] = (yn * normw_ref[:, sl] * _silu(hg_ref[:, sl])).astype(o_ref.dtype)


def _gla(hh, s_all, s_acc, p, i, *, b, n, c, valid, row0):
    t, blk0 = b * n * c, row0 // c
    H, K = HG_HEADS, HG_HEAD_DIM
    W = HG_WIDTH
    tok = lambda j: pl.BlockSpec((c, W), lambda bb, ii: (blk0 + bb * n + ii, j))
    full2 = lambda a: pl.BlockSpec(a.shape, lambda bb, ii: (0, 0))
    lb = p["hg_lower_bounds"]
    normw = p["hg_norm_w"][i].reshape(1, W)
    n_in = 7
    kern, s_in, s_spec, s_out_spec, s_shape, acc_args, acc_specs, aliases = _layer_state_io(
        functools.partial(_gla_kernel, c=c, valid=valid, layer=i), n_in, s_all, s_acc, i, H * K, K)
    o, s_new = pl.pallas_call(
        kern,
        grid=(b, n),
        in_specs=[tok(0), tok(1), tok(2), tok(3), s_spec, full2(lb), full2(normw)] + acc_specs,
        out_specs=[pl.BlockSpec((c, W), lambda bb, ii: (bb * n + ii, 0)), s_out_spec],
        out_shape=[jax.ShapeDtypeStruct((t, W), BF16), s_shape],
        input_output_aliases=aliases,
        compiler_params=_params(("arbitrary", "arbitrary")),
        name="gla_scan",
    )(hh, hh, hh, hh, s_in, lb, normw, *acc_args)
    return o, s_new


def _split_bf16(x):
    hi = x.astype(BF16)
    return hi, (x - hi.astype(F32)).astype(BF16)


def _gdn_kernel(qkv_ref, z_ref, ba_ref, aT2_ref, buf_ref, s0_ref, convw_ref,
                dtb_ref, dtb2_ref, alog_ref, alog2_ref, normw_ref,
                o_ref, s_ref, bufout_ref, xpad_ref, *, c, valid, wave):
    HV, HK, K, V = GDN_V_HEADS, GDN_K_HEADS, GDN_HEAD_K, GDN_HEAD_V
    L = 2 * c
    log2c = int(math.log2(c))

    @pl.when(pl.program_id(1) == 0)
    def _():
        s_ref[...] = s0_ref[...]

    _conv_begin(xpad_ref, qkv_ref, buf_ref, c)
    rmask = _iota2((c, 1), 0) < valid

    def conv_silu(col0):
        t = _silu(_conv_tile(xpad_ref, convw_ref, c, col0, K))
        return jnp.where(rmask, t, 0.0) if valid < c else t

    row = _iota2((c, c), 0)
    col = _iota2((c, c), 1)
    ltri = (col <= row).astype(F32)
    rowp = _iota2((c, L), 0)
    lanep = _iota2((c, L), 1)
    left = lanep < c
    sp = jnp.bitwise_and(lanep, c - 1)
    tri2 = sp <= rowp
    strict2 = sp < rowp
    eye2 = (sp == rowp).astype(F32)
    rl = _iota2((L, L), 0)
    cl = _iota2((L, L), 1)
    utri2 = jnp.where(jnp.right_shift(rl, log2c) == jnp.right_shift(cl, log2c),
                      (rl <= cl).astype(F32), 0.0)

    ba = ba_ref[...]
    beta = _sigmoid(ba[:, :HV])
    gl = -jnp.exp(alog_ref[...]) * _softplus(ba[:, HV:] + dtb_ref[...])
    glT2 = -jnp.exp(alog2_ref[...]) * _softplus(aT2_ref[...] + dtb2_ref[...])
    if valid < c:
        cmask = jnp.bitwise_and(_iota2((1, L), 1), c - 1) < valid
        beta = jnp.where(rmask, beta, 0.0)
        gl = jnp.where(rmask, gl, 0.0)
        glT2 = jnp.where(cmask, glT2, 0.0)
    gcum = _sel_dot(ltri, gl)
    gcumT2 = _dot_sel(glT2, utri2)
    egc = jnp.exp(gcum)
    glast = gcum[c - 1:c, :]
    elast = jnp.exp(glast)
    edec = jnp.exp(glast - gcum)

    n_sq = log2c - 1
    normw = normw_ref[...]
    zeros_cv = jnp.zeros((c, V), F32)

    def pair_cols(a, hk):
        return jnp.where(left, a[:, 2 * hk:2 * hk + 1], a[:, 2 * hk + 1:2 * hk + 2])

    def block_diag(x2):
        return jnp.concatenate([jnp.where(left, x2, 0.0), jnp.where(left, 0.0, x2)], axis=0)

    stack_bf16 = c % (2 * SUBLANES) == 0
    left_b16 = left.astype(BF16)
    right_b16 = 1.0 - left_b16

    def split_pair(x2):
        hi, lo = _split_bf16(x2)
        if stack_bf16:
            bd = tuple(jnp.concatenate([t * left_b16, t * right_b16], axis=0) for t in (hi, lo))
        else:
            bd = _split_bf16(block_diag(x2))
        return (hi, lo), bd

    def block_diag_b16(x2):
        if not stack_bf16:
            return block_diag(x2).astype(BF16)
        xb = x2.astype(BF16)
        return jnp.concatenate([xb * left_b16, xb * right_b16], axis=0)

    def pair_product(lhs_split, rhs_bd):
        lh, ll = lhs_split
        rh, rlo = rhs_bd
        return (jnp.dot(lh, rh, preferred_element_type=F32)
                + jnp.dot(ll, rh, preferred_element_type=F32)
                + jnp.dot(lh, rlo, preferred_element_type=F32))

    for w0 in range(0, HK, wave):
        hks = list(range(w0, min(w0 + wave, HK)))
        qn, kn, gam, qkg, pw, tm = {}, {}, {}, {}, {}, {}
        for hk in hks:
            qh = conv_silu(hk * K)
            kh = conv_silu(GDN_KEY_DIM + hk * K)
            qn[hk] = qh * lax.rsqrt(jnp.sum(qh * qh, axis=-1, keepdims=True) + EPS) * (K ** -0.5)
            kn[hk] = kh * lax.rsqrt(jnp.sum(kh * kh, axis=-1, keepdims=True) + EPS)
            gq = _dot_nt(jnp.concatenate([kn[hk], qn[hk]], axis=0),
                         jnp.concatenate([kn[hk], kn[hk]], axis=0))
            diff = pair_cols(gcum, hk) - gcumT2[hk:hk + 1, :]
            gam[hk] = jnp.where(tri2, jnp.exp(jnp.where(tri2, diff, 0.0)), 0.0)
            pw[hk] = -jnp.where(strict2, pair_cols(beta, hk) * gq[:c] * gam[hk], 0.0)
            qkg[hk] = gq[c:] * gam[hk]
            tm[hk] = eye2 + pw[hk]
        pw_split = {hk: split_pair(pw[hk]) for hk in hks}
        for _ in range(n_sq):
            for hk in hks:
                pw_split[hk] = split_pair(pair_product(*pw_split[hk]))
            for hk in hks:
                tm[hk] = tm[hk] + pair_product(_split_bf16(tm[hk]), pw_split[hk][1])
        wu, ws_qs, v_new = {}, {}, {}
        for hk in hks:
            rhs = []
            for j in range(2):
                h = 2 * hk + j
                bh = beta[:, h:h + 1]
                vh = conv_silu(2 * GDN_KEY_DIM + h * V)
                rhs.append(jnp.concatenate([kn[hk] * (bh * egc[:, h:h + 1]), vh * bh], axis=1))
            wu[hk] = _dot(block_diag_b16(tm[hk]), jnp.concatenate(rhs, axis=0))
        for hk in hks:
            for j in range(2):
                h = 2 * hk + j
                wq = jnp.concatenate([wu[hk][j * c:(j + 1) * c, :K], qn[hk] * egc[:, h:h + 1]], axis=0)
                ws_qs[h] = _dot(wq, s_ref[h * K:(h + 1) * K, :])
                v_new[h] = wu[hk][j * c:(j + 1) * c, K:] - ws_qs[h][:c]
        for hk in hks:
            ha, hb = 2 * hk, 2 * hk + 1
            y2 = _dot(block_diag_b16(qkg[hk]), jnp.concatenate([v_new[ha], v_new[hb]], axis=0))
            kdec = jnp.concatenate([kn[hk] * edec[:, ha:ha + 1], kn[hk] * edec[:, hb:hb + 1]], axis=0)
            vbd = jnp.concatenate([jnp.concatenate([v_new[ha], zeros_cv], axis=1),
                                   jnp.concatenate([zeros_cv, v_new[hb]], axis=1)], axis=0)
            upd = _dot_tn(kdec, vbd)
            for j, h in enumerate((ha, hb)):
                sl = slice(h * K, (h + 1) * K)
                s_ref[sl, :] = elast[:, h:h + 1] * s_ref[sl, :] + upd[:, j * V:(j + 1) * V]
                y = ws_qs[h][c:] + y2[j * c:(j + 1) * c]
                yn = (y * lax.rsqrt(jnp.mean(y * y, axis=-1, keepdims=True) + EPS)) * normw
                o_ref[:, h * V:(h + 1) * V] = (yn * _silu(z_ref[:, h * V:(h + 1) * V])).astype(o_ref.dtype)
    _conv_carry(xpad_ref, bufout_ref, c, valid)


def _pair_packed_transpose(a, b, n, c):
    hh = a.shape[-1] // 2
    x = a.reshape(b, n, c, hh, 2)
    return jnp.transpose(x, (0, 1, 3, 4, 2)).reshape(b, n, hh, 2 * c)


def _gdn(qkv, z, ba, buf, s_all, s_acc, p, i, *, b, n, c, valid, row0):
    t, blk0 = b * n * c, row0 // c
    HV, HK, K, V = GDN_V_HEADS, GDN_K_HEADS, GDN_HEAD_K, GDN_HEAD_V
    C = GDN_CONV_DIM
    tok = lambda w: pl.BlockSpec((c, w), lambda bb, ii: (blk0 + bb * n + ii, 0))
    tok_out = lambda w: pl.BlockSpec((c, w), lambda bb, ii: (bb * n + ii, 0))
    full2 = lambda a: pl.BlockSpec(a.shape, lambda bb, ii: (0, 0))
    aT2 = _pair_packed_transpose(ba[row0:row0 + t, HV:], b, n, c)
    convw = p["gdn_conv_w"][i]
    dtb = p["gdn_dt_bias"][i].reshape(1, HV)
    dtb2 = jnp.repeat(p["gdn_dt_bias"][i].reshape(HK, 2), c, axis=1)
    alog = p["gdn_A_log"][i].reshape(1, HV)
    alog2 = jnp.repeat(p["gdn_A_log"][i].reshape(HK, 2), c, axis=1)
    normw = p["gdn_norm_w"][i].reshape(1, V)
    small = [convw, dtb, dtb2, alog, alog2, normw]
    n_in = 6 + len(small)
    wave = GDN_WAVE if c >= CHUNK else GDN_WAVE_SHORT_CHUNK
    kern, s_in, s_spec, s_out_spec, s_shape, acc_args, acc_specs, aliases = _layer_state_io(
        functools.partial(_gdn_kernel, c=c, valid=valid, wave=wave), n_in, s_all, s_acc, i, HV * K, V)
    o, s_new, buf_new = pl.pallas_call(
        kern,
        grid=(b, n),
        in_specs=[tok(C), tok(GDN_VAL_DIM), tok(2 * HV),
                  pl.BlockSpec((None, None, HK, 2 * c), lambda bb, ii: (bb, ii, 0, 0)),
                  pl.BlockSpec((None, CONV_W - 1, C), lambda bb, ii: (bb, 0, 0)),
                  s_spec]
                 + [full2(a) for a in small] + acc_specs,
        out_specs=[tok_out(GDN_VAL_DIM), s_out_spec,
                   pl.BlockSpec((None, CONV_W - 1, C), lambda bb, ii: (bb, 0, 0))],
        out_shape=[jax.ShapeDtypeStruct((t, GDN_VAL_DIM), BF16), s_shape,
                   jax.ShapeDtypeStruct((b, CONV_W - 1, C), F32)],
        scratch_shapes=[pltpu.VMEM((8 + c, C), F32)],
        input_output_aliases=aliases,
        compiler_params=_params(("arbitrary", "arbitrary")),
        name="gdn_scan",
    )(qkv, z, ba, aT2, buf, s_in, *small, *acc_args)
    return o, s_new, buf_new


def _gla_dims(dims):
    if dims["c"] != CHUNK or dims["valid"] != CHUNK or dims["n"] % 2 or dims["row0"] % (2 * CHUNK):
        return dims
    return dict(dims, c=2 * CHUNK, valid=2 * CHUNK, n=dims["n"] // 2)


def _trunk(groups, p):
    d = groups[0]["x"].shape[-1]
    row0 = 0
    for gr in groups:
        b, lp, _ = gr["x"].shape
        gr["dims"] = dict(b=b, n=lp // gr["c"], c=gr["c"], valid=gr["valid"], row0=row0)
        gr["rows"] = b * lp
        gr["new"] = dict(ssm=None, hg=None, gdn=None, ssm_conv=[], gdn_conv=[])
        row0 += b * lp
    xt = jnp.concatenate([gr["x"].reshape(gr["rows"], d) for gr in groups], axis=0)
    for l in range(DEPTH):
        i = l // 2
        if l % 2 == 0:
            h, dt = _rmsnorm_proj(xt, p["norm_mix"][l], p["w_in_ab_t"], i, AB_OFF_DT, SSM_HEADS)
            in_proj = functools.partial(_matmul, h, p["w_in_ab_t"], i, w_rows_are_outputs=True)
            z = in_proj(0, SSM_D_INNER)
            xbc = in_proj(AB_OFF_XBC, SSM_CONV_DIM)
            hh = in_proj(AB_OFF_HG, 4 * HG_WIDTH)
            ys, os = [], []
            for gr in groups:
                new = gr["new"]
                y, new["ssm"], b1 = _ssd(z, xbc, dt, gr["ssm_conv"][i], gr["ssm"], new["ssm"],
                                         p, i, **gr["dims"])
                o, new["hg"] = _gla(hh, gr["hg"], new["hg"], p, i, **_gla_dims(gr["dims"]))
                new["ssm_conv"].append(b1)
                ys.append(y)
                os.append(o)
            xt = _matmul((ys, os), p["w_out_ab"], i, 0, D_MODEL, epilogue="residual", residual=xt)
        else:
            h, ba = _rmsnorm_proj(xt, p["norm_mix"][l], p["w_in_c_t"], i, C_OFF_BA, 2 * GDN_V_HEADS)
            in_proj = functools.partial(_matmul, h, p["w_in_c_t"], i, w_rows_are_outputs=True)
            qkv = in_proj(0, GDN_CONV_DIM)
            z = in_proj(C_OFF_Z, GDN_VAL_DIM)
            os = []
            for gr in groups:
                new = gr["new"]
                o, new["gdn"], b3 = _gdn(qkv, z, ba, gr["gdn_conv"][i], gr["gdn"], new["gdn"],
                                         p, i, **gr["dims"])
                new["gdn_conv"].append(b3)
                os.append(o)
            xt = _matmul((os,), p["w_out_c"], i, 0, D_MODEL, epilogue="residual", residual=xt)
        h = _rmsnorm(xt, p["norm_mlp"][l], BF16)
        u = _matmul(h, p["w_ff1"], l, 0, FFN_HIDDEN, epilogue="relu2", out_dtype=BF16)
        xt = _matmul(u, p["w_ff2"], l, 0, D_MODEL, epilogue="residual", residual=xt)
    outs = []
    for gr in groups:
        new = gr["new"]
        y = _rmsnorm(xt, p["norm_final"], F32, gr["dims"]["row0"], gr["rows"]).reshape(gr["x"].shape)
        outs.append((y, new["ssm"].reshape(gr["ssm"].shape), jnp.stack(new["ssm_conv"]),
                     new["hg"].reshape(gr["hg"].shape), new["gdn"].reshape(gr["gdn"].shape),
                     jnp.stack(new["gdn_conv"])))
    return outs


def kernel(x_prompt, x_sample, state_ssm, state_ssm_conv, state_hgrn, state_gdn, state_gdn_conv, norm_mix, norm_mlp, norm_final, w_in_ab, ssm_conv_w, ssm_conv_b, ssm_dt_bias, ssm_A_log, ssm_D, ssm_norm_w, hg_lower_bounds, hg_norm_w, w_out_ab, w_in_c, gdn_conv_w, gdn_dt_bias, gdn_A_log, gdn_norm_w, w_out_c, w_ff1, w_ff2):
    p = dict(norm_mix=norm_mix, norm_mlp=norm_mlp, norm_final=norm_final, w_in_ab=w_in_ab,
             ssm_conv_w=ssm_conv_w, ssm_conv_b=ssm_conv_b, ssm_dt_bias=ssm_dt_bias, ssm_A_log=ssm_A_log,
             ssm_D=ssm_D, ssm_norm_w=ssm_norm_w, hg_lower_bounds=hg_lower_bounds, hg_norm_w=hg_norm_w,
             w_out_ab=w_out_ab, w_in_c=w_in_c, gdn_conv_w=gdn_conv_w, gdn_dt_bias=gdn_dt_bias,
             gdn_A_log=gdn_A_log, gdn_norm_w=gdn_norm_w, w_out_c=w_out_c, w_ff1=w_ff1, w_ff2=w_ff2)
    p["w_in_ab_t"] = jnp.swapaxes(w_in_ab, 1, 2)
    p["w_in_c_t"] = jnp.swapaxes(w_in_c, 1, 2)

    bp, seq, _ = x_prompt.shape
    z_ssm = jnp.zeros((N_AB, bp, SSM_HEADS, SSM_HEAD_DIM, SSM_D_STATE), F32)
    z_ssm_conv = jnp.zeros((N_AB, bp, CONV_W - 1, SSM_CONV_DIM), F32)
    z_hg = jnp.zeros((N_AB, bp, HG_HEADS, HG_HEAD_DIM, HG_HEAD_DIM), F32)
    z_gdn = jnp.zeros((N_C, bp, GDN_V_HEADS, GDN_HEAD_K, GDN_HEAD_V), F32)
    z_gdn_conv = jnp.zeros((N_C, bp, CONV_W - 1, GDN_CONV_DIM), F32)
    c_p = min(CHUNK, seq)
    prompt = dict(x=x_prompt, c=c_p, valid=c_p, ssm=z_ssm, ssm_conv=z_ssm_conv, hg=z_hg, gdn=z_gdn,
                  gdn_conv=z_gdn_conv)

    bs, dec, _ = x_sample.shape
    c_s = -(-dec // SUBLANES) * SUBLANES
    xs_pad = jnp.pad(x_sample, ((0, 0), (0, c_s - dec), (0, 0)))
    sample = dict(x=xs_pad, c=c_s, valid=dec, ssm=state_ssm, ssm_conv=state_ssm_conv, hg=state_hgrn,
                  gdn=state_gdn, gdn_conv=state_gdn_conv)
    outs_p, outs_s = _trunk([prompt, sample], p)
    y_sample = outs_s[0][:, :dec]
    return (outs_p[0], y_sample) + tuple(outs_p[1:]) + tuple(outs_s[1:])
```
